```python
import math
import jax
import jax.numpy as jnp
from jax import lax
import numpy as np

D_MODEL = 1024
BATCH = 2
SEQ = 8192
DEPTH = 4
DEC_BATCH = 32
DEC_SEQ = 1
PAST_LEN = 8192
PAGE_SIZE = 128

HEAD_DIM = 64
NSA_H = 8
NSA_KVH = 2
NSA_GROUP = NSA_H // NSA_KVH
FOX_H = 8
SB_H = 16
NSA_W = NSA_H * HEAD_DIM
NSA_KV_W = NSA_KVH * HEAD_DIM
FOX_W = FOX_H * HEAD_DIM
SB_W = SB_H * HEAD_DIM
CMP_BLOCK = 64
SEL_BLOCK = 64
TOP_N = 16
WINDOW = 512
Q_BLOCK = 128
N_BUCKETS = 32
REL_MAX_DIST = 128
FORGET_BIAS = 2.0
N_EVEN = (DEPTH + 1) // 2
N_ODD = DEPTH // 2
EVEN_SIZES = (NSA_W, NSA_KV_W, NSA_KV_W, NSA_KV_W, NSA_KV_W, NSA_KV_W, NSA_KV_W, 3 * NSA_H, NSA_W,
              FOX_W, FOX_W, FOX_W, FOX_H, FOX_W)
EVEN_COLS = sum(EVEN_SIZES)
ODD_COLS = 4 * SB_W
RMS_EPS = 1e-6
NEG_INF = -1e30

kernel_name = 'nsa_fox_stickbreak_hybrid_step'


def rms_norm(x, g):
    xf = x.astype(jnp.float32)
    y = xf * lax.rsqrt(jnp.mean(xf * xf, axis=-1, keepdims=True) + RMS_EPS)
    return (y * g.astype(jnp.float32)).astype(x.dtype)


def masked_softmax(logits, mask):
    logits = jnp.where(mask, logits, NEG_INF)
    m = jnp.max(logits, axis=-1, keepdims=True)
    e = jnp.where(mask, jnp.exp(logits - m), 0.0)
    return e / jnp.maximum(jnp.sum(e, axis=-1, keepdims=True), 1e-30)


def rel_bucket(dist):
    dist = jnp.maximum(dist, 0)
    max_exact = N_BUCKETS // 2
    large = max_exact + (jnp.log(jnp.maximum(dist, 1).astype(jnp.float32) / max_exact)
                         / math.log(REL_MAX_DIST / max_exact) * (N_BUCKETS - max_exact)).astype(jnp.int32)
    return jnp.where(dist < max_exact, dist, jnp.minimum(large, N_BUCKETS - 1))


def _split(u, sizes):
    out, o = [], 0
    for s in sizes:
        out.append(u[..., o:o + s])
        o += s
    return out


def _join_blocks(o):
    o = jnp.moveaxis(o, 0, 1)
    return o.reshape(o.shape[0], -1, *o.shape[3:])


def compress(k_rows, v_rows, pos_emb, w1, w2):
    B, L = k_rows.shape[:2]
    nb = L // CMP_BLOCK

    def phi(rows, j):
        blocks = rows[:, :nb * CMP_BLOCK].reshape(B, nb, CMP_BLOCK, NSA_KVH, HEAD_DIM) + pos_emb[j][:, None, :]
        flat = jnp.swapaxes(blocks, 2, 3).reshape(B, nb, NSA_KVH, CMP_BLOCK * HEAD_DIM)
        return jax.nn.silu(flat @ w1[j]) @ w2[j]

    c_end = jnp.arange(nb) * CMP_BLOCK + (CMP_BLOCK - 1)
    return phi(k_rows, 0), phi(v_rows, 1), c_end


def nsa_attend(q, q_pos, kc, vc, c_end, fetch_sel, n_sel, kw, vw, w_pos, gates, rel_bias):
    B, Tq = q.shape[:2]
    f32 = jnp.float32
    scale = HEAD_DIM ** -0.5
    qg = q.reshape(B, Tq, NSA_KVH, NSA_GROUP, HEAD_DIM)
    tab = rel_bias.reshape(N_BUCKETS, NSA_KVH, NSA_GROUP)

    dist_c = q_pos[:, None] - c_end[None, :]
    bias_c = jnp.transpose(tab[rel_bucket(dist_c)], (0, 2, 3, 1))
    s_c = jnp.einsum('bqgjd,bngd->bqgjn', qg, kc).astype(f32) * scale + bias_c
    p_c = masked_softmax(s_c, (dist_c >= 0)[:, None, None, :])
    o_c = jnp.einsum('bqgjn,bngd->bqgjn'[:0] + 'bqgjn,bngd->bqgjd', p_c.astype(vc.dtype), vc)

    nb_c = kc.shape[1]
    score = jnp.pad(jnp.sum(p_c, axis=3), ((0, 0), (0, 0), (0, 0), (0, n_sel - nb_c)))
    blk = jnp.arange(n_sel)[None, :]
    cur = (q_pos // SEL_BLOCK)[:, None]
    forced = (blk == 0) | (blk == cur) | (blk == cur - 1)
    score = jnp.where(forced[:, None, :], NSA_GROUP + 1.0,
                      jnp.where((blk <= cur)[:, None, :], score, -1.0))
    n_top = min(TOP_N, n_sel)
    _, sel = lax.top_k(score, n_top)
    sel_pos = (sel[..., None] * SEL_BLOCK + jnp.arange(SEL_BLOCK)).reshape(B, Tq, NSA_KVH, n_top * SEL_BLOCK)
    k_s, v_s = fetch_sel(sel_pos)
    dist_s = q_pos[None, :, None, None] - sel_pos
    bias_s = jnp.swapaxes(tab[rel_bucket(dist_s), jnp.arange(NSA_KVH)[:, None]], -1, -2)
    s_s = jnp.einsum('bqgjd,bqgmd->bqgjm', qg, k_s).astype(f32) * scale + bias_s
    p_s = masked_softmax(s_s, (dist_s >= 0)[:, :, :, None, :])
    o_s = jnp.einsum('bqgjm,bqgmd->bqgjd', p_s.astype(v_s.dtype), v_s)

    dist_w = q_pos[:, None] - w_pos[None, :]
    mask_w = (w_pos[None, :] >= 0) & (dist_w >= 0) & (dist_w <= WINDOW)
    bias_w = jnp.transpose(tab[rel_bucket(dist_w)], (0, 2, 3, 1))
    s_w = jnp.einsum('bqgjd,bkgd->bqgjk', qg, kw).astype(f32) * scale + bias_w
    p_w = masked_softmax(s_w, mask_w[:, None, None, :])
    o_w = jnp.einsum('bqgjk,bkgd->bqgjd', p_w.astype(vw.dtype), vw)

    g = jax.nn.sigmoid(gates.astype(f32)).reshape(B, Tq, NSA_KVH, NSA_GROUP, 3)
    o = g[..., 0:1] * o_c + g[..., 1:2] * o_s + g[..., 2:3] * o_w
    return o.reshape(B, Tq, NSA_W).astype(q.dtype)


def fox_attend(q, k, v, cq, ck, q_pos, k_pos):
    s = jnp.einsum('bqhd,bkhd->bhqk', q, k).astype(jnp.float32) * (HEAD_DIM ** -0.5)
    s = s + (jnp.transpose(cq, (0, 2, 1))[..., :, None] - jnp.transpose(ck, (0, 2, 1))[..., None, :])
    p = masked_softmax(s, k_pos[None, :] <= q_pos[:, None])
    return jnp.einsum('bhqk,bkhd->bqhd', p.astype(v.dtype), v)


def sb_attend(q, k, v, q_pos, k_pos):
    z = jnp.einsum('bqhd,bkhd->bhqk', q, k).astype(jnp.float32) * (HEAD_DIM ** -0.5)
    mask = k_pos[None, :] < q_pos[:, None]
    log_1mb = jnp.where(mask, jax.nn.log_sigmoid(-z), 0.0)
    after = lax.cumsum(log_1mb, axis=3, reverse=True) - log_1mb
    a = jnp.where(mask, jnp.exp(jax.nn.log_sigmoid(z) + after), 0.0)
    return jnp.einsum('bhqk,bkhd->bqhd', a.astype(v.dtype), v)


def even_mix_in(x, g, w_in, b_f):
    B, T = x.shape[:2]
    u = rms_norm(x, g) @ w_in
    (nq, kc, vc, ks, vs, kw, vw, ng, nz, fq, fk, fv, ff, fz) = _split(u, EVEN_SIZES)
    hd = lambda a, n: a.reshape(B, T, n, HEAD_DIM)
    logf = jax.nn.log_sigmoid((ff + b_f).astype(jnp.float32))
    return (hd(nq, NSA_H), hd(kc, NSA_KVH), hd(vc, NSA_KVH), hd(ks, NSA_KVH), hd(vs, NSA_KVH),
            hd(kw, NSA_KVH), hd(vw, NSA_KVH), ng, nz, hd(fq, FOX_H), hd(fk, FOX_H), hd(fv, FOX_H), logf, fz)


def even_mix_out(x, o_nsa, z_nsa, o_fox, z_fox, w_out):
    B, T = x.shape[:2]
    m = jnp.concatenate([o_nsa.reshape(B, T, NSA_W) * jax.nn.silu(z_nsa),
                         o_fox.reshape(B, T, FOX_W) * jax.nn.silu(z_fox)], axis=-1)
    return x + m @ w_out


def nsa_prompt(q, kc_rows, vc_rows, ks, vs, kw, vw, gates, cmp_pos, w1, w2, rel_bias):
    B, S = q.shape[:2]
    kc, vc, c_end = compress(kc_rows, vc_rows, cmp_pos, w1, w2)
    n_sel = S // SEL_BLOCK
    b_idx = jnp.arange(B)[:, None, None, None]
    g_idx = jnp.arange(NSA_KVH)[None, None, :, None]

    def fetch(pos):
        return ks[b_idx, pos, g_idx], vs[b_idx, pos, g_idx]

    pad = ((0, 0), (WINDOW, 0), (0, 0), (0, 0))
    kw_pad, vw_pad = jnp.pad(kw, pad), jnp.pad(vw, pad)

    def blk(i):
        st = i * Q_BLOCK
        sl = lambda a, n: lax.dynamic_slice_in_dim(a, st, n, 1)
        return nsa_attend(sl(q, Q_BLOCK), st + jnp.arange(Q_BLOCK), kc, vc, c_end, fetch, n_sel,
                          sl(kw_pad, WINDOW + Q_BLOCK), sl(vw_pad, WINDOW + Q_BLOCK),
                          st - WINDOW + jnp.arange(WINDOW + Q_BLOCK), sl(gates, Q_BLOCK), rel_bias)

    return _join_blocks(lax.map(blk, jnp.arange(S // Q_BLOCK)))


def nsa_sample(q, kc_new, vc_new, ks_new, vs_new, kw_new, vw_new, gates, cmp_pos, w1, w2, rel_bias,
               cache_nsa, win_buf, page_table, li):
    Bd, T = q.shape[:2]
    past_c = cache_nsa[page_table, li, :, :2].reshape(Bd, -1, 2, NSA_KVH, HEAD_DIM)
    P = past_c.shape[1]
    L = P + T
    kc, vc, c_end = compress(jnp.concatenate([past_c[:, :, 0], kc_new], 1),
                             jnp.concatenate([past_c[:, :, 1], vc_new], 1), cmp_pos, w1, w2)
    n_sel = -(-L // SEL_BLOCK)
    b_idx = jnp.arange(Bd)[:, None, None, None]
    g_idx = jnp.arange(NSA_KVH)[None, None, :, None]

    def fetch(pos):
        in_past = (pos < P)[..., None]
        pp = jnp.clip(pos, 0, P - 1)
        phys = page_table[b_idx, pp // PAGE_SIZE]
        row = pp % PAGE_SIZE
        jn = jnp.clip(pos - P, 0, T - 1)
        k = jnp.where(in_past, cache_nsa[phys, li, row, 2, g_idx], ks_new[b_idx, jn, g_idx])
        v = jnp.where(in_past, cache_nsa[phys, li, row, 3, g_idx], vs_new[b_idx, jn, g_idx])
        return k, v

    wb = win_buf.shape[1]
    kw_all = jnp.concatenate([win_buf[:, :, 0], kw_new], 1)
    vw_all = jnp.concatenate([win_buf[:, :, 1], vw_new], 1)
    w_pos = P - wb + jnp.arange(wb + T)
    o = nsa_attend(q, P + jnp.arange(T), kc, vc, c_end, fetch, n_sel, kw_all, vw_all, w_pos, gates, rel_bias)
    new_win = jnp.stack([kw_all, vw_all], 2)[:, T:]
    return o, new_win


def fox_prompt(q, k, v, logf):
    B, S = q.shape[:2]
    c = jnp.cumsum(logf, axis=1)
    pos = jnp.arange(S)

    def blk(i):
        st = i * Q_BLOCK
        return fox_attend(lax.dynamic_slice_in_dim(q, st, Q_BLOCK, 1), k, v,
                          lax.dynamic_slice_in_dim(c, st, Q_BLOCK, 1), c, st + jnp.arange(Q_BLOCK), pos)

    return _join_blocks(lax.map(blk, jnp.arange(S // Q_BLOCK)))


def fox_sample(q, k, v, logf, cache_fox, cache_fox_logf, page_table, li):
    Bd, T = q.shape[:2]
    past = cache_fox[page_table, li].reshape(Bd, -1, 2, FOX_H, HEAD_DIM)
    k_all = jnp.concatenate([past[:, :, 0], k], 1)
    v_all = jnp.concatenate([past[:, :, 1], v], 1)
    lf_past = cache_fox_logf[page_table, li].reshape(Bd, -1, FOX_H).astype(jnp.float32)
    c = jnp.cumsum(jnp.concatenate([lf_past, logf], 1), axis=1)
    L = k_all.shape[1]
    return fox_attend(q, k_all, v_all, c[:, L - T:], c, jnp.arange(L - T, L), jnp.arange(L))


def even_layer_prompt(x, g, w_in, w_out, b_f, cmp_pos, w1, w2, rel_bias):
    (nq, kc, vc, ks, vs, kw, vw, ng, nz, fq, fk, fv, lf, fz) = even_mix_in(x, g, w_in, b_f)
    o_n = nsa_prompt(nq, kc, vc, ks, vs, kw, vw, ng, cmp_pos, w1, w2, rel_bias)
    o_f = fox_prompt(fq, fk, fv, lf)
    y = even_mix_out(x, o_n, nz, o_f, fz, w_out)
    S = x.shape[1]
    win = jnp.stack([kw, vw], 2)[:, S - min(WINDOW, S):]
    return y, jnp.stack([kc, vc, ks, vs], 2), win, jnp.stack([fk, fv], 2), lf


def even_layer_sample(x, g, w_in, w_out, b_f, cmp_pos, w1, w2, rel_bias,
                      cache_nsa, win_buf, cache_fox, cache_fox_logf, page_table, li):
    (nq, kc, vc, ks, vs, kw, vw, ng, nz, fq, fk, fv, lf, fz) = even_mix_in(x, g, w_in, b_f)
    o_n, new_win = nsa_sample(nq, kc, vc, ks, vs, kw, vw, ng, cmp_pos, w1, w2, rel_bias,
                              cache_nsa, win_buf, page_table, li)
    o_f = fox_sample(fq, fk, fv, lf, cache_fox, cache_fox_logf, page_table, li)
    y = even_mix_out(x, o_n, nz, o_f, fz, w_out)
    return y, jnp.stack([kc, vc, ks, vs], 2), new_win, jnp.stack([fk, fv], 2), lf


def odd_mix_in(x, g, w_in):
    B, T = x.shape[:2]
    q, k, v, z = _split(rms_norm(x, g) @ w_in, (SB_W, SB_W, SB_W, SB_W))
    hd = lambda a: a.reshape(B, T, SB_H, HEAD_DIM)
    return hd(q), hd(k), hd(v), z


def odd_mix_out(x, o, z, w_out):
    B, T = x.shape[:2]
    return x + (o.reshape(B, T, SB_W) * jax.nn.silu(z)) @ w_out


def odd_layer_prompt(x, g, w_in, w_out):
    q, k, v, z = odd_mix_in(x, g, w_in)
    S = x.shape[1]
    pos = jnp.arange(S)

    def blk(i):
        st = i * Q_BLOCK
        return sb_attend(lax.dynamic_slice_in_dim(q, st, Q_BLOCK, 1), k, v, st + jnp.arange(Q_BLOCK), pos)

    o = _join_blocks(lax.map(blk, jnp.arange(S // Q_BLOCK)))
    return odd_mix_out(x, o, z, w_out), jnp.stack([k, v], 2)


def odd_layer_sample(x, g, w_in, w_out, cache_sb, page_table, li):
    q, k, v, z = odd_mix_in(x, g, w_in)
    Bd, T = x.shape[:2]
    past = cache_sb[page_table, li].reshape(Bd, -1, 2, SB_H, HEAD_DIM)
    k_all = jnp.concatenate([past[:, :, 0], k], 1)
    v_all = jnp.concatenate([past[:, :, 1], v], 1)
    L = k_all.shape[1]
    o = sb_attend(q, k_all, v_all, jnp.arange(L - T, L), jnp.arange(L))
    return odd_mix_out(x, o, z, w_out), jnp.stack([k, v], 2)


def setup_inputs(seed: int = 0) -> dict:
    key = jax.random.key(seed)
    k = jax.random.split(key, 19)
    f32 = jnp.float32
    n_pages = PAST_LEN // PAGE_SIZE
    n_phys = (DEC_BATCH * n_pages * 5) // 4
    win_buf = min(WINDOW, PAST_LEN)
    nrm = lambda kk, shape, scale: scale * jax.random.normal(kk, shape, f32)
    page_table = jax.random.permutation(k[7], n_phys)[: DEC_BATCH * n_pages].reshape(DEC_BATCH, n_pages).astype(jnp.int32)
    return {
        'x_prompt': jax.random.normal(k[0], (BATCH, SEQ, D_MODEL), f32),
        'x_sample': jax.random.normal(k[1], (DEC_BATCH, DEC_SEQ, D_MODEL), f32),
        'cache_nsa': jax.random.normal(k[2], (n_phys, N_EVEN, PAGE_SIZE, 4, NSA_KVH, HEAD_DIM), f32),
        'cache_nsa_win': jax.random.normal(k[3], (DEC_BATCH, N_EVEN, win_buf, 2, NSA_KVH, HEAD_DIM), f32),
        'cache_fox': jax.random.normal(k[4], (n_phys, N_EVEN, PAGE_SIZE, 2, FOX_H, HEAD_DIM), f32),
        'cache_fox_logf': jax.nn.log_sigmoid(FORGET_BIAS + jax.random.normal(k[5], (n_phys, N_EVEN, PAGE_SIZE, FOX_H), f32)),
        'cache_sb': jax.random.normal(k[6], (n_phys, N_ODD, PAGE_SIZE, 2, SB_H, HEAD_DIM), f32),
        'page_table': page_table,
        'norm_g': 1.0 + nrm(k[8], (DEPTH, D_MODEL), 0.02),
        'final_g': 1.0 + nrm(k[9], (D_MODEL,), 0.02),
        'rel_bias': nrm(k[10], (N_BUCKETS, NSA_H), 0.3),
        'w_in_even': nrm(k[11], (N_EVEN, D_MODEL, EVEN_COLS), D_MODEL ** -0.5),
        'w_out_even': nrm(k[12], (N_EVEN, NSA_W + FOX_W, D_MODEL), (NSA_W + FOX_W) ** -0.5),
        'b_forget': FORGET_BIAS + nrm(k[13], (N_EVEN, FOX_H), 0.5),
        'cmp_pos': nrm(k[14], (N_EVEN, 2, CMP_BLOCK, HEAD_DIM), 0.1),
        'w_cmp1': nrm(k[15], (N_EVEN, 2, CMP_BLOCK * HEAD_DIM, HEAD_DIM), (CMP_BLOCK * HEAD_DIM) ** -0.5),
        'w_cmp2': nrm(k[16], (N_EVEN, 2, HEAD_DIM, HEAD_DIM), HEAD_DIM ** -0.5),
        'w_in_odd': nrm(k[17], (N_ODD, D_MODEL, ODD_COLS), D_MODEL ** -0.5),
        'w_out_odd': nrm(k[18], (N_ODD, SB_W, D_MODEL), SB_W ** -0.5),
    }


def reference(x_prompt, x_sample, cache_nsa, cache_nsa_win, cache_fox, cache_fox_logf, cache_sb, page_table,
              norm_g, final_g, rel_bias, w_in_even, w_out_even, b_forget, cmp_pos, w_cmp1, w_cmp2,
              w_in_odd, w_out_odd):
    xp, xs = x_prompt, x_sample
    nsa_p, nsa_s, win_p, win_s, fox_p, fox_s, lf_p, lf_s, sb_p, sb_s = ([] for _ in range(10))
    for layer in range(DEPTH):
        g = norm_g[layer]
        li = layer // 2
        if layer % 2 == 0:
            w = (w_in_even[li], w_out_even[li], b_forget[li], cmp_pos[li], w_cmp1[li], w_cmp2[li])
            xp, rows, win, fkv, lf = even_layer_prompt(xp, g, *w, rel_bias)
            nsa_p.append(rows); win_p.append(win); fox_p.append(fkv); lf_p.append(lf)
            xs, rows, win, fkv, lf = even_layer_sample(xs, g, *w, rel_bias, cache_nsa, cache_nsa_win[:, li],
                                                       cache_fox, cache_fox_logf, page_table, li)
            nsa_s.append(rows); win_s.append(win); fox_s.append(fkv); lf_s.append(lf)
        else:
            xp, kv = odd_layer_prompt(xp, g, w_in_odd[li], w_out_odd[li])
            sb_p.append(kv)
            xs, kv = odd_layer_sample(xs, g, w_in_odd[li], w_out_odd[li], cache_sb, page_table, li)
            sb_s.append(kv)
    y_prompt = rms_norm(xp, final_g)
    y_sample = rms_norm(xs, final_g)
    nsa_rows_prompt, nsa_rows_sample = jnp.stack(nsa_p, 1), jnp.stack(nsa_s, 1)
    nsa_win_prompt, nsa_win_sample = jnp.stack(win_p, 1), jnp.stack(win_s, 1)
    fox_rows_prompt, fox_rows_sample = jnp.stack(fox_p, 1), jnp.stack(fox_s, 1)
    fox_logf_prompt, fox_logf_sample = jnp.stack(lf_p, 1), jnp.stack(lf_s, 1)
    sb_rows_prompt, sb_rows_sample = jnp.stack(sb_p, 1), jnp.stack(sb_s, 1)
    return (y_prompt, y_sample, nsa_rows_prompt, nsa_rows_sample, nsa_win_prompt, nsa_win_sample,
            fox_rows_prompt, fox_rows_sample, fox_logf_prompt, fox_logf_sample, sb_rows_prompt, sb_rows_sample)
```

```python
import functools
import math

import numpy as np
import jax
import jax.numpy as jnp
from jax import lax
from jax.experimental import pallas as pl
from jax.experimental.pallas import tpu as pltpu

F32 = jnp.float32
BF16 = jnp.bfloat16

D_MODEL = 1024
HEAD_DIM = 64
LANES = 128
NSA_H = 8
NSA_KVH = 2
NSA_GROUP = NSA_H // NSA_KVH
FOX_H = 8
SB_H = 16
NSA_W = NSA_H * HEAD_DIM
FOX_W = FOX_H * HEAD_DIM
SB_W = SB_H * HEAD_DIM
CMP_BLOCK = 64
TOP_N = 16
WINDOW = 512
PAGE_SIZE = 128
N_BUCKETS = 32
REL_MAX_DIST = 128
RMS_EPS = 1e-6
NEG_INF = -1e30
SCALE = HEAD_DIM ** -0.5

ATT_T = 256
VMEM_LIMIT = 56 * 1024 * 1024


def _cparams(*sem):
    return pltpu.CompilerParams(dimension_semantics=sem, vmem_limit_bytes=VMEM_LIMIT)


def _dot(a, b):
    return jnp.dot(a, b, preferred_element_type=F32)


def _dot_nt(a, b):
    return lax.dot_general(a, b, (((1,), (1,)), ((), ())), preferred_element_type=F32)


def _split2(x):
    hi = x.astype(BF16)
    lo = (x - hi.astype(F32)).astype(BF16)
    return hi, lo


def _dot_exact01(x, w01, passes, left=False):
    acc = None
    r = x
    for _ in range(passes):
        piece = r.astype(BF16)
        term = _dot(w01, piece) if left else _dot(piece, w01)
        acc = term if acc is None else acc + term
        r = r - piece.astype(F32)
    return acc


def _softplus(z):
    return jnp.maximum(z, 0.0) + jnp.log1p(jnp.exp(-jnp.abs(z)))


def _sigmoid(z):
    return 1.0 / (1.0 + jnp.exp(-z))


def _rms(x, g):
    return x * lax.rsqrt(jnp.mean(x * x, axis=-1, keepdims=True) + RMS_EPS) * g


def _lane_lo(shape):
    return lax.broadcasted_iota(jnp.int32, shape, len(shape) - 1) < HEAD_DIM


def _dup_halves(x):
    r = pltpu.roll(x, HEAD_DIM, 1)
    lo = _lane_lo(x.shape)
    return jnp.where(lo, x, r), jnp.where(lo, r, x)


E_Q, E_ROWS, E_WIN, E_NZ, E_FQ, E_FK, E_FV, E_FZ, E_SMALL, E_END = (
    0, 512, 1024, 1280, 1792, 2304, 2816, 3328, 3840, 3968)
SMALL_GATE = 3 * NSA_H


def _even_in_kernel(x_ref, g_ref, w_ref, bf_ref, q_ref, rows_ref, selkv_ref, win_ref, winkv_ref, small_ref,
                    nz_ref, fq_ref, fkv_ref, fk_ref, fv_ref, fz_ref):
    xn = _rms(x_ref[...], g_ref[...]).astype(BF16)

    def mm(a, b):
        return _dot(xn, w_ref[:, a:b])

    q_ref[...] = (mm(E_Q, E_ROWS) * SCALE).astype(BF16)
    rows = mm(E_ROWS, E_WIN)
    rows_ref[...] = rows
    kd0, kd1 = _dup_halves(rows[:, 256:384])
    vd0, vd1 = _dup_halves(rows[:, 384:512])
    selkv_ref[...] = jnp.concatenate([kd0, kd1, vd0, vd1], axis=1).astype(BF16)
    win = mm(E_WIN, E_NZ)
    win_ref[...] = win
    kd0, kd1 = _dup_halves(win[:, 0:128])
    vd0, vd1 = _dup_halves(win[:, 128:256])
    winkv_ref[...] = jnp.concatenate([kd0, kd1, vd0, vd1], axis=1).astype(BF16)
    nz_ref[...] = mm(E_NZ, E_FQ)
    fq_ref[...] = (mm(E_FQ, E_FK) * SCALE).astype(BF16)
    fkv = mm(E_FK, E_FZ)
    fkv_ref[...] = fkv
    fk_ref[...] = fkv[:, :FOX_W].astype(BF16)
    fv_ref[...] = fkv[:, FOX_W:].astype(BF16)
    fz_ref[...] = mm(E_FZ, E_SMALL)
    small = mm(E_SMALL, E_END)
    lane = lax.broadcasted_iota(jnp.int32, small.shape, 1)
    is_f = (lane >= SMALL_GATE) & (lane < SMALL_GATE + FOX_H)
    small_ref[...] = jnp.where(is_f, -_softplus(-(small + bf_ref[...])), small)


def _even_in(x, g, w, bfp, tm):
    m = x.shape[0]
    row = lambda c: pl.BlockSpec((tm, c), lambda i: (i, 0))
    full = lambda a: pl.BlockSpec(a.shape, lambda i: (0, 0))
    outs = [(512, BF16), (512, F32), (512, BF16), (256, F32), (512, BF16), (128, F32),
            (512, F32), (512, BF16), (1024, F32), (512, BF16), (512, BF16), (512, F32)]
    return pl.pallas_call(
        _even_in_kernel,
        grid=(m // tm,),
        in_specs=[row(D_MODEL), full(g), full(w), full(bfp)],
        out_specs=[row(c) for c, _ in outs],
        out_shape=[jax.ShapeDtypeStruct((m, c), dt) for c, dt in outs],
        compiler_params=_cparams("parallel"),
        name="even_in",
    )(x, g, w, bfp)


def _odd_in_kernel(x_ref, g_ref, w_ref, q_ref, kv_ref, k_ref, v_ref, z_ref):
    xn = _rms(x_ref[...], g_ref[...]).astype(BF16)
    q_ref[...] = (_dot(xn, w_ref[:, 0:SB_W]) * SCALE).astype(BF16)
    kv = _dot(xn, w_ref[:, SB_W:3 * SB_W])
    kv_ref[...] = kv
    k_ref[...] = kv[:, :SB_W].astype(BF16)
    v_ref[...] = kv[:, SB_W:].astype(BF16)
    z_ref[...] = _dot(xn, w_ref[:, 3 * SB_W:])


def _odd_in(x, g, w, tm):
    m = x.shape[0]
    row = lambda c: pl.BlockSpec((tm, c), lambda i: (i, 0))
    full = lambda a: pl.BlockSpec(a.shape, lambda i: (0, 0))
    outs = [(SB_W, BF16), (2 * SB_W, F32), (SB_W, BF16), (SB_W, BF16), (SB_W, F32)]
    return pl.pallas_call(
        _odd_in_kernel,
        grid=(m // tm,),
        in_specs=[row(D_MODEL), full(g), full(w)],
        out_specs=[row(c) for c, _ in outs],
        out_shape=[jax.ShapeDtypeStruct((m, c), dt) for c, dt in outs],
        compiler_params=_cparams("parallel"),
        name="odd_in",
    )(x, g, w)


def _out_kernel(n_parts, final, *refs):
    x_ref = refs[0]
    parts = refs[1:1 + 2 * n_parts]
    w_ref = refs[1 + 2 * n_parts]
    gf_ref = refs[2 + 2 * n_parts]
    y_ref = refs[3 + 2 * n_parts]
    y = x_ref[...]
    off = 0
    for p in range(n_parts):
        o = parts[2 * p][...]
        z = parts[2 * p + 1][...]
        k = o.shape[1]
        y = y + _dot((o * (z * _sigmoid(z))).astype(BF16), w_ref[off:off + k, :])
        off += k
    if final:
        y = _rms(y, gf_ref[...])
    y_ref[...] = y


def _out_proj(x, parts, w, gf, final, tm):
    m = x.shape[0]
    row = lambda c: pl.BlockSpec((tm, c), lambda i: (i, 0))
    full = lambda a: pl.BlockSpec(a.shape, lambda i: (0, 0))
    flat = [a for pr in parts for a in pr]
    return pl.pallas_call(
        functools.partial(_out_kernel, len(parts), final),
        grid=(m // tm,),
        in_specs=[row(D_MODEL)] + [row(a.shape[1]) for a in flat] + [full(w), full(gf)],
        out_specs=row(D_MODEL),
        out_shape=jax.ShapeDtypeStruct((m, D_MODEL), F32),
        compiler_params=_cparams("parallel"),
        name="out_proj",
    )(x, *flat, w, gf)


def _cumsum_kernel(x_ref, o_ref, *, chunk):
    n = x_ref.shape[2] // chunk
    r = lax.broadcasted_iota(jnp.int32, (chunk, chunk), 0)
    c = lax.broadcasted_iota(jnp.int32, (chunk, chunk), 1)
    tri = jnp.where(r <= c, 1.0, 0.0).astype(BF16)

    def body(j, carry):
        x = x_ref[0, :, pl.ds(j * chunk, chunk)]
        cs = _dot_exact01(x, tri, 3) + carry
        o_ref[0, :, pl.ds(j * chunk, chunk)] = cs
        return cs[:, chunk - 1:chunk]

    lax.fori_loop(0, n, body, jnp.zeros((x_ref.shape[1], 1), F32))


def _cumsum_lanes(x):
    b, h, s = x.shape
    spec = pl.BlockSpec((1, h, s), lambda i: (i, 0, 0))
    return pl.pallas_call(
        functools.partial(_cumsum_kernel, chunk=min(256, s)),
        grid=(b,),
        in_specs=[spec],
        out_specs=spec,
        out_shape=jax.ShapeDtypeStruct(x.shape, F32),
        compiler_params=_cparams("parallel"),
        name="logf_cumsum",
    )(x)


def _fox_prompt_kernel(q_ref, k_ref, v_ref, c_ref, o_ref, *, t):
    i = pl.program_id(2)
    q2 = q_ref[0]
    lo = _lane_lo(q2.shape)
    zero = jnp.zeros_like(q2)
    qs = (jnp.where(lo, q2, zero), jnp.where(lo, zero, q2))
    row = lax.broadcasted_iota(jnp.int32, (t, t), 0)
    col = lax.broadcasted_iota(jnp.int32, (t, t), 1)
    causal = row >= col

    def step(j, carry, masked):
        kblk = k_ref[0, pl.ds(j * t, t), :]
        vblk = v_ref[0, pl.ds(j * t, t), :]
        cb = c_ref[0, 0, :, pl.ds(j * t, t)]
        new = []
        for h in range(2):
            m_old, l_old, acc = carry[h]
            s = _dot_nt(qs[h], kblk) - cb[h:h + 1, :]
            if masked:
                s = jnp.where(causal, s, NEG_INF)
            m_new = jnp.maximum(m_old, jnp.max(s, axis=1, keepdims=True))
            alpha = jnp.exp(m_old - m_new)
            p = jnp.exp(s - m_new)
            if masked:
                p = jnp.where(causal, p, 0.0)
            l_new = alpha * l_old + jnp.sum(p, axis=1, keepdims=True)
            acc = alpha * acc + _dot(p.astype(BF16), vblk)
            new.append((m_new, l_new, acc))
        return tuple(new)

    init = tuple((jnp.full((t, 1), NEG_INF, F32), jnp.zeros((t, 1), F32), jnp.zeros((t, LANES), F32))
                 for _ in range(2))
    carry = lax.fori_loop(0, i, lambda j, c: step(j, c, False), init)
    carry = step(i, carry, True)
    outs = [acc / jnp.maximum(l, 1e-30) for (_, l, acc) in carry]
    o_ref[0] = jnp.where(lo, outs[0], outs[1])


def _fox_prompt(q, k, v, c4, t):
    b, s, w = q.shape
    hp = w // LANES
    return pl.pallas_call(
        functools.partial(_fox_prompt_kernel, t=t),
        grid=(b, hp, s // t),
        in_specs=[pl.BlockSpec((1, t, LANES), lambda bi, h, i: (bi, i, h)),
                  pl.BlockSpec((1, s, LANES), lambda bi, h, i: (bi, 0, h)),
                  pl.BlockSpec((1, s, LANES), lambda bi, h, i: (bi, 0, h)),
                  pl.BlockSpec((1, 1, 2, s), lambda bi, h, i: (bi, h, 0, 0))],
        out_specs=pl.BlockSpec((1, t, LANES), lambda bi, h, i: (bi, i, h)),
        out_shape=jax.ShapeDtypeStruct((b, s, w), F32),
        compiler_params=_cparams("parallel", "parallel", "arbitrary"),
        name="fox_prompt",
    )(q, k, v, c4)


def _sb_weights(z, after_carry, ustrict, mask):
    sp = _softplus(z)
    l1m = -sp if mask is None else jnp.where(mask, -sp, 0.0)
    after = _dot_exact01(l1m, ustrict, 2) + after_carry
    a = jnp.exp(z - sp + after)
    if mask is not None:
        a = jnp.where(mask, a, 0.0)
    return a, after_carry + jnp.sum(l1m, axis=1, keepdims=True)


def _sb_prompt_kernel(q_ref, k_ref, v_ref, o_ref, *, t):
    i = pl.program_id(2)
    q2 = q_ref[0]
    lo = _lane_lo(q2.shape)
    zero = jnp.zeros_like(q2)
    qs = (jnp.where(lo, q2, zero), jnp.where(lo, zero, q2))
    row = lax.broadcasted_iota(jnp.int32, (t, t), 0)
    col = lax.broadcasted_iota(jnp.int32, (t, t), 1)
    strict = row > col
    ustrict = jnp.where(strict, 1.0, 0.0).astype(BF16)

    def step(j, carry, mask):
        kblk = k_ref[0, pl.ds(j * t, t), :]
        vblk = v_ref[0, pl.ds(j * t, t), :]
        new = []
        for h in range(2):
            run, acc = carry[h]
            a, run = _sb_weights(_dot_nt(qs[h], kblk), run, ustrict, mask)
            new.append((run, acc + _dot(a.astype(BF16), vblk)))
        return tuple(new)

    init = tuple((jnp.zeros((t, 1), F32), jnp.zeros((t, LANES), F32)) for _ in range(2))
    carry = step(i, init, strict)
    carry = lax.fori_loop(0, i, lambda jj, c: step(i - 1 - jj, c, None), carry)
    o_ref[0] = jnp.where(lo, carry[0][1], carry[1][1])


def _sb_prompt(q, k, v, t):
    b, s, w = q.shape
    hp = w // LANES
    return pl.pallas_call(
        functools.partial(_sb_prompt_kernel, t=t),
        grid=(b, hp, s // t),
        in_specs=[pl.BlockSpec((1, t, LANES), lambda bi, h, i: (bi, i, h)),
                  pl.BlockSpec((1, s, LANES), lambda bi, h, i: (bi, 0, h)),
                  pl.BlockSpec((1, s, LANES), lambda bi, h, i: (bi, 0, h))],
        out_specs=pl.BlockSpec((1, t, LANES), lambda bi, h, i: (bi, i, h)),
        out_shape=jax.ShapeDtypeStruct((b, s, w), F32),
        compiler_params=_cparams("parallel", "parallel", "arbitrary"),
        name="sb_prompt",
    )(q, k, v)


def _bucket_starts():
    d = np.arange(0, 4 * REL_MAX_DIST)
    max_exact = N_BUCKETS // 2
    large = max_exact + (np.log(np.maximum(d, 1).astype(np.float32) / max_exact)
                         / math.log(REL_MAX_DIST / max_exact) * (N_BUCKETS - max_exact)).astype(np.int32)
    b = np.where(d < max_exact, d, np.minimum(large, N_BUCKETS - 1))
    return [int(np.min(np.nonzero(b >= k)[0])) for k in range(N_BUCKETS)]


BUCKET_STARTS = _bucket_starts()
BIAS_FAR_DIST = BUCKET_STARTS[-1]


def _bias_from_dist(dist, tab):
    v = jnp.where(dist >= BUCKET_STARTS[1], tab(1), tab(0))
    for b in range(2, N_BUCKETS):
        v = jnp.where(dist >= BUCKET_STARTS[b], tab(b), v)
    return v


def _masked_softmax(s, mask):
    logits = jnp.where(mask, s, NEG_INF)
    m = jnp.max(logits, axis=-1, keepdims=True)
    e = jnp.where(mask, jnp.exp(logits - m), 0.0)
    return e / jnp.maximum(jnp.sum(e, axis=-1, keepdims=True), 1e-30)


def _top_blocks(score, n_top):
    lanef = lax.broadcasted_iota(jnp.int32, score.shape, 1).astype(F32)
    work = score
    msel = jnp.zeros(score.shape, F32)
    for _ in range(n_top):
        mx = jnp.max(work, axis=1, keepdims=True)
        idx = jnp.min(jnp.where(work == mx, lanef, 1e9), axis=1, keepdims=True)
        pick = lanef == idx
        msel = jnp.where(pick, 1.0, msel)
        work = jnp.where(pick, -2.0, work)
    return msel


def _compress_rows(k_ref, v_ref, nb, pe_ref, w1_ref, w2_ref):
    outs = []
    for kv, ref in enumerate((k_ref, v_ref)):
        acc = jnp.zeros((nb, LANES), F32)
        for pos in range(CMP_BLOCK):
            x = ref[pl.ds(pos, nb, stride=CMP_BLOCK), :] + pe_ref[kv, pos:pos + 1, :]
            acc = acc + _dot(x.astype(BF16), w1_ref[kv, pos])
        hid = acc * _sigmoid(acc)
        outs.append(_dot(hid.astype(BF16), w2_ref[kv]))
    return outs


def _compress_kernel(k_ref, v_ref, pe_ref, w1_ref, w2_ref, o_ref):
    nb = o_ref.shape[1]
    kc, vc = _compress_rows(k_ref.at[0], v_ref.at[0], nb, pe_ref, w1_ref, w2_ref)
    o_ref[0] = jnp.concatenate([kc, vc], axis=1).astype(BF16)


def _compress_prompt(rows, pe, w1bd, w2big):
    b, s, _ = rows.shape
    nb = s // CMP_BLOCK
    full = lambda a: pl.BlockSpec(a.shape, lambda i: (0,) * a.ndim)
    return pl.pallas_call(
        _compress_kernel,
        grid=(b,),
        in_specs=[pl.BlockSpec((1, s, LANES), lambda i: (i, 0, 0)), pl.BlockSpec((1, s, LANES), lambda i: (i, 0, 1)),
                  full(pe), full(w1bd), full(w2big)],
        out_specs=pl.BlockSpec((1, nb, 4 * LANES), lambda i: (i, 0, 0)),
        out_shape=jax.ShapeDtypeStruct((b, nb, 4 * LANES), BF16),
        compiler_params=_cparams("parallel"),
        name="nsa_compress_prompt",
    )(rows, rows, pe, w1bd, w2big)


def _nsa_prompt_kernel(tab_ref, q_ref, ks_ref, vs_ref, kw_ref, vw_ref, kc_ref, vc_ref, gate_ref, o_ref, tiles_sc,
                       *, t, n_top):
    g = pl.program_id(1)
    i = pl.program_id(2)
    nh = NSA_GROUP
    nb = kc_ref.shape[1]
    row = lax.broadcasted_iota(jnp.int32, (t, t), 0)
    col = lax.broadcasted_iota(jnp.int32, (t, t), 1)

    @pl.when(i == 0)
    def _():
        for hh in range(nh):
            h = g * nh + hh
            for kind in range(2):
                tiles_sc[kind, hh] = _bias_from_dist(kind * t + row - col, lambda b: tab_ref[b, h])
            tiles_sc[2, hh] = jnp.full((t, t), tab_ref[N_BUCKETS - 1, h], F32)

    q4 = q_ref[0]
    lo = _lane_lo((t, LANES))
    zero = jnp.zeros((t, LANES), BF16)
    qa, qb = q4[:, :LANES], q4[:, LANES:]
    qst = jnp.concatenate([jnp.where(lo, qa, zero), jnp.where(lo, zero, qa),
                           jnp.where(lo, qb, zero), jnp.where(lo, zero, qb)], axis=0)

    rowc = lax.broadcasted_iota(jnp.int32, (t, nb), 0)
    colc = lax.broadcasted_iota(jnp.int32, (t, nb), 1)
    dist_c = (i * t + rowc) - (colc * CMP_BLOCK + (CMP_BLOCK - 1))
    bias_c = jnp.stack([_bias_from_dist(dist_c, functools.partial(lambda b, h: tab_ref[b, h], h=g * nh + hh))
                        for hh in range(nh)], axis=0)
    valid_c = (dist_c >= 0)[None]
    p_c = _masked_softmax(_dot_nt(qst, kc_ref[0]).reshape(nh, t, nb) + bias_c, valid_c)
    o_c = _dot(p_c.reshape(nh * t, nb).astype(BF16), vc_ref[0])

    cur = (i * t + rowc) // CMP_BLOCK
    forced = (colc == 0) | (colc == cur) | (colc == cur - 1)
    score = jnp.where(forced, NSA_GROUP + 1.0, jnp.where(colc <= cur, jnp.sum(p_c, axis=0), -1.0))
    msel = _top_blocks(score, n_top).astype(BF16)

    blk_of_key = lax.broadcasted_iota(jnp.int32, (nb, t), 0)
    key_in_tile = lax.broadcasted_iota(jnp.int32, (nb, t), 1)

    def flash_step(j, carry, k_ref, v_ref, selected):
        m_old, l_old, acc = carry
        kblk = k_ref[0, pl.ds(j * t, t), :]
        vblk = v_ref[0, pl.ds(j * t, t), :]
        dist = (i - j) * t + row - col
        if selected:
            expand = jnp.where(blk_of_key == jnp.right_shift(j * t + key_in_tile, 6), 1.0, 0.0).astype(BF16)
            mask = (_dot(msel, expand) > 0.5) & (dist >= 0)
        else:
            mask = (dist >= 0) & (dist <= WINDOW)
        mask = mask[None]
        s = _dot_nt(qst, kblk).reshape(nh, t, t) + tiles_sc[jnp.minimum(i - j, 2)]
        s = jnp.where(mask, s, NEG_INF)
        m_new = jnp.maximum(m_old, jnp.max(s, axis=-1, keepdims=True))
        alpha = jnp.exp(m_old - m_new)
        p = jnp.where(mask, jnp.exp(s - m_new), 0.0)
        l_new = alpha * l_old + jnp.sum(p, axis=-1, keepdims=True)
        acc = alpha.reshape(nh * t, 1) * acc + _dot(p.reshape(nh * t, t).astype(BF16), vblk)
        return m_new, l_new, acc

    def finish(carry):
        _, l, acc = carry
        return acc / jnp.maximum(l, 1e-30).reshape(nh * t, 1)

    init = (jnp.full((nh, t, 1), NEG_INF, F32), jnp.zeros((nh, t, 1), F32), jnp.zeros((nh * t, LANES), F32))
    o_s = finish(lax.fori_loop(0, i + 1, lambda j, c: flash_step(j, c, ks_ref, vs_ref, True), init))
    first_w = jnp.maximum(i - (WINDOW + t - 1) // t, 0)
    o_w = finish(lax.fori_loop(first_w, i + 1, lambda j, c: flash_step(j, c, kw_ref, vw_ref, False), init))

    sig = _sigmoid(gate_ref[0])
    lane = lax.broadcasted_iota(jnp.int32, sig.shape, 1)

    def gate(hh, br):
        return jnp.sum(jnp.where(lane == (g * nh + hh) * 3 + br, sig, 0.0), axis=1, keepdims=True)

    outs = []
    for hh in range(nh):
        sl = slice(hh * t, (hh + 1) * t)
        outs.append(gate(hh, 0) * o_c[sl] + gate(hh, 1) * o_s[sl] + gate(hh, 2) * o_w[sl])
    o_ref[0] = jnp.concatenate([jnp.where(lo, outs[0], outs[1]), jnp.where(lo, outs[2], outs[3])], axis=1)


def _nsa_prompt(tab, q, selkv, winkv, cmp, small, t):
    b, s, _ = q.shape
    nb = cmp.shape[1]
    assert t >= BIAS_FAR_DIST and s % t == 0
    kv = lambda off: pl.BlockSpec((1, s, LANES), lambda bi, g, i: (bi, 0, off + g))
    cm = lambda off: pl.BlockSpec((1, nb, LANES), lambda bi, g, i: (bi, 0, off + g))
    return pl.pallas_call(
        functools.partial(_nsa_prompt_kernel, t=t, n_top=min(TOP_N, nb)),
        grid=(b, NSA_KVH, s // t),
        in_specs=[pl.BlockSpec(memory_space=pltpu.SMEM),
                  pl.BlockSpec((1, t, 2 * LANES), lambda bi, g, i: (bi, i, g)),
                  kv(0), kv(2), kv(0), kv(2), cm(0), cm(2),
                  pl.BlockSpec((1, t, LANES), lambda bi, g, i: (bi, i, 0))],
        out_specs=pl.BlockSpec((1, t, 2 * LANES), lambda bi, g, i: (bi, i, g)),
        out_shape=jax.ShapeDtypeStruct((b, s, NSA_W), F32),
        scratch_shapes=[pltpu.VMEM((3, NSA_GROUP, t, t), F32)],
        compiler_params=_cparams("parallel", "parallel", "arbitrary"),
        name="nsa_prompt",
    )(tab, q, selkv, selkv, winkv, winkv, cmp, cmp, small)


def _head_diag(x, width):
    r = lax.broadcasted_iota(jnp.int32, x.shape, 0)
    c = lax.broadcasted_iota(jnp.int32, x.shape, 1)
    own = (c >= r * width) & (c < (r + 1) * width)
    return jnp.sum(jnp.where(own, x, 0.0), axis=0, keepdims=True)


def _fox_decode_kernel(pt_ref, q_ref, page_ref, lf_ref, kvn_ref, lfn_ref, o_ref, m_sc, l_sc, acc_sc, c_sc):
    p = pl.program_id(1)
    npg = pl.num_programs(1)
    w = FOX_W

    @pl.when(p == 0)
    def _():
        m_sc[...] = jnp.full(m_sc.shape, NEG_INF, F32)
        l_sc[...] = jnp.zeros(l_sc.shape, F32)
        acc_sc[...] = jnp.zeros(acc_sc.shape, F32)
        c_sc[...] = jnp.zeros(c_sc.shape, F32)

    q = q_ref[0]
    r = lax.broadcasted_iota(jnp.int32, (PAGE_SIZE, PAGE_SIZE), 0)
    cidx = lax.broadcasted_iota(jnp.int32, (PAGE_SIZE, PAGE_SIZE), 1)
    tri = jnp.where(r <= cidx, 1.0, 0.0).astype(BF16)
    kv = page_ref[0, 0]
    c = _dot_exact01(lf_ref[0, 0], tri, 3) + c_sc[...]
    c_sc[...] = c[:, PAGE_SIZE - 1:PAGE_SIZE]
    s = _dot_nt(q, kv[:, :w].astype(BF16)) - c
    m_new = jnp.maximum(m_sc[...], jnp.max(s, axis=1, keepdims=True))
    alpha = jnp.exp(m_sc[...] - m_new)
    e = jnp.exp(s - m_new)
    l_sc[...] = alpha * l_sc[...] + jnp.sum(e, axis=1, keepdims=True)
    acc_sc[...] = alpha * acc_sc[...] + _dot(e.astype(BF16), kv[:, w:].astype(BF16))
    m_sc[...] = m_new

    @pl.when(p == npg - 1)
    def _():
        kvn = kvn_ref[0].astype(BF16).astype(F32)
        s_n = jnp.sum(q.astype(F32) * kvn[:, :w], axis=1, keepdims=True) - (c_sc[...] + lfn_ref[0])
        m_fin = jnp.maximum(m_sc[...], s_n)
        a = jnp.exp(m_sc[...] - m_fin)
        e_n = jnp.exp(s_n - m_fin)
        l_fin = a * l_sc[...] + e_n
        acc = a * acc_sc[...] + e_n.astype(BF16).astype(F32) * kvn[:, w:]
        o_ref[0] = _head_diag(acc / jnp.maximum(l_fin, 1e-30), HEAD_DIM)


def _fox_decode(page_table, qrows, cache, lft, kv_new, lf_new, li):
    bd, npg = page_table.shape
    h = FOX_H
    grid_spec = pltpu.PrefetchScalarGridSpec(
        num_scalar_prefetch=1,
        grid=(bd, npg),
        in_specs=[pl.BlockSpec((1, h, FOX_W), lambda b, p, pt: (b, 0, 0)),
                  pl.BlockSpec((1, 1, PAGE_SIZE, 2 * FOX_W), lambda b, p, pt: (pt[b, p], li, 0, 0)),
                  pl.BlockSpec((1, 1, h, PAGE_SIZE), lambda b, p, pt: (pt[b, p], li, 0, 0)),
                  pl.BlockSpec((1, 1, 2 * FOX_W), lambda b, p, pt: (b, 0, 0)),
                  pl.BlockSpec((1, h, 1), lambda b, p, pt: (b, 0, 0))],
        out_specs=pl.BlockSpec((1, 1, FOX_W), lambda b, p, pt: (b, 0, 0)),
        scratch_shapes=[pltpu.VMEM((h, 1), F32), pltpu.VMEM((h, 1), F32), pltpu.VMEM((h, FOX_W), F32),
                        pltpu.VMEM((h, 1), F32)])
    return pl.pallas_call(
        _fox_decode_kernel,
        grid_spec=grid_spec,
        out_shape=jax.ShapeDtypeStruct((bd, 1, FOX_W), F32),
        compiler_params=_cparams("parallel", "arbitrary"),
        name="fox_decode",
    )(page_table, qrows, cache, lft, kv_new, lf_new)


def _sb_decode_kernel(pt_ref, q_ref, page_ref, o_ref, run_sc, acc_sc):
    p = pl.program_id(1)

    @pl.when(p == 0)
    def _():
        run_sc[...] = jnp.zeros(run_sc.shape, F32)
        acc_sc[...] = jnp.zeros(acc_sc.shape, F32)

    r = lax.broadcasted_iota(jnp.int32, (PAGE_SIZE, PAGE_SIZE), 0)
    cidx = lax.broadcasted_iota(jnp.int32, (PAGE_SIZE, PAGE_SIZE), 1)
    ustrict = jnp.where(r > cidx, 1.0, 0.0).astype(BF16)
    kv = page_ref[0, 0]
    a, run = _sb_weights(_dot_nt(q_ref[0], kv[:, :SB_W].astype(BF16)), run_sc[...], ustrict, None)
    run_sc[...] = run
    acc_sc[...] = acc_sc[...] + _dot(a.astype(BF16), kv[:, SB_W:].astype(BF16))

    @pl.when(p == pl.num_programs(1) - 1)
    def _():
        o_ref[0] = _head_diag(acc_sc[...], HEAD_DIM)


def _sb_decode(page_table, qrows, cache, li):
    bd, npg = page_table.shape
    grid_spec = pltpu.PrefetchScalarGridSpec(
        num_scalar_prefetch=1,
        grid=(bd, npg),
        in_specs=[pl.BlockSpec((1, SB_H, SB_W), lambda b, p, pt: (b, 0, 0)),
                  pl.BlockSpec((1, 1, PAGE_SIZE, 2 * SB_W), lambda b, p, pt: (pt[b, npg - 1 - p], li, 0, 0))],
        out_specs=pl.BlockSpec((1, 1, SB_W), lambda b, p, pt: (b, 0, 0)),
        scratch_shapes=[pltpu.VMEM((SB_H, 1), F32), pltpu.VMEM((SB_H, SB_W), F32)])
    return pl.pallas_call(
        _sb_decode_kernel,
        grid_spec=grid_spec,
        out_shape=jax.ShapeDtypeStruct((bd, 1, SB_W), F32),
        compiler_params=_cparams("parallel", "arbitrary"),
        name="sb_decode",
    )(page_table, qrows, cache)


def _nsa_decode_kernel(pt_ref, q_ref, page_ref, win_ref, new_ref, gate_ref, tab_ref, expand_ref, pe_ref, w1_ref,
                       w2_ref, o_ref, kc_sc, vc_sc, ks_sc, vs_sc, *, n_top):
    p = pl.program_id(1)
    npg = pl.num_programs(1)
    past = kc_sc.shape[0]
    nb = past // CMP_BLOCK
    page = page_ref[0, 0]
    rows = pl.ds(pl.multiple_of(p * PAGE_SIZE, PAGE_SIZE), PAGE_SIZE)
    kc_sc[rows, :] = page[:, 0:LANES]
    vc_sc[rows, :] = page[:, LANES:2 * LANES]
    ks_sc[rows, :] = page[:, 2 * LANES:3 * LANES].astype(BF16)
    vs_sc[rows, :] = page[:, 3 * LANES:].astype(BF16)

    @pl.when(p == npg - 1)
    def _():
        q = q_ref[0]
        qf = q.astype(F32)
        tab = lambda b: tab_ref[:, b:b + 1]
        new = new_ref[0].astype(BF16).astype(F32)

        def with_new(s, mask, k_new, v_new, v_past):
            s_n = jnp.sum(qf * k_new, axis=1, keepdims=True) + tab(0)
            s = jnp.where(mask, s, NEG_INF)
            m = jnp.maximum(jnp.max(s, axis=1, keepdims=True), s_n)
            e = jnp.where(mask, jnp.exp(s - m), 0.0)
            e_n = jnp.exp(s_n - m)
            l = jnp.sum(e, axis=1, keepdims=True) + e_n
            acc = _dot(e.astype(BF16), v_past) + e_n.astype(BF16).astype(F32) * v_new
            return acc / jnp.maximum(l, 1e-30)

        kc, vc = _compress_rows(kc_sc, vc_sc, nb, pe_ref, w1_ref, w2_ref)
        colc = lax.broadcasted_iota(jnp.int32, (NSA_H, nb), 1)
        dist_c = past - (colc * CMP_BLOCK + (CMP_BLOCK - 1))
        s_c = _dot_nt(q, kc.astype(BF16)) + _bias_from_dist(dist_c, tab)
        p_c = _masked_softmax(s_c, dist_c >= 0)
        o_c = _dot(p_c.astype(BF16), vc.astype(BF16))

        rg = lax.broadcasted_iota(jnp.int32, (NSA_H, NSA_H), 0) // NSA_GROUP
        cg = lax.broadcasted_iota(jnp.int32, (NSA_H, NSA_H), 1) // NSA_GROUP
        same_group = jnp.where(rg == cg, 1.0, 0.0).astype(BF16)
        score = _dot_exact01(p_c, same_group, 3, left=True)
        cur = past // CMP_BLOCK
        forced = (colc == 0) | (colc == cur) | (colc == cur - 1)
        score = jnp.where(forced, NSA_GROUP + 1.0, jnp.where(colc <= cur, score, -1.0))
        msel = _top_blocks(score, n_top - 1).astype(BF16)

        cols = lax.broadcasted_iota(jnp.int32, (NSA_H, past), 1)
        dist_s = past - cols
        s_s = _dot_nt(q, ks_sc[...]) + _bias_from_dist(dist_s, tab)
        o_s = with_new(s_s, _dot(msel, expand_ref[...]) > 0.5, new[0:1], new[1:2], vs_sc[...])

        win = win_ref[0, 0]
        wb = win.shape[0]
        colw = lax.broadcasted_iota(jnp.int32, (NSA_H, wb), 1)
        dist_w = wb - colw
        s_w = _dot_nt(q, win[:, :LANES].astype(BF16)) + _bias_from_dist(dist_w, tab)
        o_w = with_new(s_w, dist_w <= WINDOW, new[2:3], new[3:4], win[:, LANES:].astype(BF16))

        gt = _sigmoid(gate_ref[0])
        o = gt[:, 0:1] * o_c + gt[:, 1:2] * o_s + gt[:, 2:3] * o_w
        first_group = lax.broadcasted_iota(jnp.int32, o.shape, 0) < NSA_GROUP
        o_ref[0] = jnp.where(first_group, o, pltpu.roll(o, HEAD_DIM, 1))


def _nsa_decode(page_table, qg, cache, win, new_rows, gates, tab_t, expand, pe, w1bd, w2bd, li):
    bd, npg = page_table.shape
    past = npg * PAGE_SIZE
    nb = past // CMP_BLOCK
    wb = win.shape[2]
    full = lambda a: pl.BlockSpec(a.shape, lambda b, p, pt: (0,) * a.ndim)
    per_seq = lambda a: pl.BlockSpec((1,) + a.shape[1:], lambda b, p, pt: (b,) + (0,) * (a.ndim - 1))
    grid_spec = pltpu.PrefetchScalarGridSpec(
        num_scalar_prefetch=1,
        grid=(bd, npg),
        in_specs=[per_seq(qg),
                  pl.BlockSpec((1, 1, PAGE_SIZE, 4 * LANES), lambda b, p, pt: (pt[b, p], li, 0, 0)),
                  pl.BlockSpec((1, 1, wb, 2 * LANES), lambda b, p, pt: (b, li, 0, 0)),
                  per_seq(new_rows), per_seq(gates), full(tab_t), full(expand), full(pe), full(w1bd), full(w2bd)],
        out_specs=pl.BlockSpec((1, NSA_H, LANES), lambda b, p, pt: (b, 0, 0)),
        scratch_shapes=[pltpu.VMEM((past, LANES), F32), pltpu.VMEM((past, LANES), F32),
                        pltpu.VMEM((past, LANES), BF16), pltpu.VMEM((past, LANES), BF16)])
    return pl.pallas_call(
        functools.partial(_nsa_decode_kernel, n_top=min(TOP_N, nb + 1)),
        grid_spec=grid_spec,
        out_shape=jax.ShapeDtypeStruct((bd, NSA_H, LANES), F32),
        compiler_params=_cparams("parallel", "arbitrary"),
        name="nsa_decode",
    )(page_table, qg, cache, win, new_rows, gates, tab_t, expand, pe, w1bd, w2bd)


def _prep_even_weights(w_in, b_f, cmp_pos, w1, w2):
    w = jnp.concatenate([w_in[:, 0:1280], w_in[:, 1304:1816], w_in[:, 1816:3352], w_in[:, 3360:3872],
                         w_in[:, 1280:1304], w_in[:, 3352:3360],
                         jnp.zeros((D_MODEL, LANES - SMALL_GATE - FOX_H), w_in.dtype)], axis=1).astype(BF16)
    bfp = jnp.zeros((1, LANES), F32).at[0, SMALL_GATE:SMALL_GATE + FOX_H].set(b_f)
    w1r = w1.reshape(2, CMP_BLOCK, HEAD_DIM, HEAD_DIM)
    z1 = jnp.zeros_like(w1r)
    w1bd = jnp.concatenate([jnp.concatenate([w1r, z1], -1), jnp.concatenate([z1, w1r], -1)], -2).astype(BF16)
    pe = jnp.concatenate([cmp_pos, cmp_pos], -1)
    z2 = jnp.zeros_like(w2)
    w2dup = jnp.concatenate([jnp.concatenate([w2, w2, z2, z2], -1), jnp.concatenate([z2, z2, w2, w2], -1)], -2).astype(BF16)
    w2bd = jnp.concatenate([jnp.concatenate([w2, z2], -1), jnp.concatenate([z2, w2], -1)], -2).astype(BF16)
    return w, bfp, pe, w1bd, w2dup, w2bd


def _head_rows(q, n_heads):
    bd = q.shape[0]
    q3 = q.reshape(bd, 1, n_heads, HEAD_DIM)
    eye = jnp.eye(n_heads, dtype=q.dtype).reshape(1, n_heads, n_heads, 1)
    return (q3 * eye).reshape(bd, n_heads, n_heads * HEAD_DIM)


def _group_rows(q):
    bd = q.shape[0]
    q3 = q.reshape(bd, NSA_H, HEAD_DIM)
    z = jnp.zeros_like(q3)
    first = (jnp.arange(NSA_H) < NSA_GROUP).reshape(1, NSA_H, 1)
    return jnp.where(first, jnp.concatenate([q3, z], -1), jnp.concatenate([z, q3], -1))


def kernel(x_prompt, x_sample, cache_nsa, cache_nsa_win, cache_fox, cache_fox_logf, cache_sb, page_table, norm_g, final_g, rel_bias, w_in_even, w_out_even, b_forget, cmp_pos, w_cmp1, w_cmp2, w_in_odd, w_out_odd):
    b, s, d = x_prompt.shape
    bd, ds_, _ = x_sample.shape
    assert ds_ == 1 and d == D_MODEL
    depth = norm_g.shape[0]
    n_even = w_in_even.shape[0]
    n_odd = w_in_odd.shape[0]
    n_phys = cache_nsa.shape[0]
    npg = page_table.shape[1]
    past = npg * PAGE_SIZE
    wb = cache_nsa_win.shape[2]
    t = min(ATT_T, s)
    tm_p = 256
    tm_o = 512

    xp = x_prompt.reshape(b * s, d)
    xs = x_sample.reshape(bd, d)
    gfin = final_g.reshape(1, d)
    cache_nsa4 = cache_nsa.reshape(n_phys, n_even, PAGE_SIZE, 4 * LANES)
    win4 = cache_nsa_win.reshape(bd, n_even, wb, 2 * LANES)
    cache_fox4 = cache_fox.reshape(n_phys, n_even, PAGE_SIZE, 2 * FOX_W)
    lft = jnp.transpose(cache_fox_logf.astype(F32), (0, 1, 3, 2))
    cache_sb4 = cache_sb.reshape(n_phys, n_odd, PAGE_SIZE, 2 * SB_W)
    tab_t = rel_bias.T
    expand = (jnp.arange(past)[None, :] // CMP_BLOCK == jnp.arange(past // CMP_BLOCK)[:, None]).astype(BF16)

    outs = {k: [] for k in ("nsa_p", "nsa_s", "win_p", "win_s", "fox_p", "fox_s", "lf_p", "lf_s", "sb_p", "sb_s")}
    for layer in range(depth):
        g = norm_g[layer].reshape(1, d)
        li = layer // 2
        final = layer == depth - 1
        if layer % 2 == 0:
            w, bfp, pe, w1bd, w2dup, w2bd = _prep_even_weights(w_in_even[li], b_forget[li], cmp_pos[li], w_cmp1[li],
                                                               w_cmp2[li])
            w_out = w_out_even[li].astype(BF16)
            q, rows, selkv, win, winkv, small, nz, fq, fkv, fk, fv, fz = _even_in(xp, g, w, bfp, tm_p)
            r3 = lambda a: a.reshape(b, s, a.shape[-1])
            cmp = _compress_prompt(r3(rows), pe, w1bd, w2dup)
            o_n = _nsa_prompt(rel_bias, r3(q), r3(selkv), r3(winkv), cmp, r3(small), t)
            logf = r3(small)[:, :, SMALL_GATE:SMALL_GATE + FOX_H]
            c = _cumsum_lanes(jnp.transpose(logf, (0, 2, 1)))
            o_f = _fox_prompt(r3(fq), r3(fk), r3(fv), c.reshape(b, FOX_H // 2, 2, s), t)
            xp = _out_proj(xp, [(o_n.reshape(b * s, NSA_W), nz), (o_f.reshape(b * s, FOX_W), fz)], w_out, gfin, final,
                           tm_o)
            outs["nsa_p"].append(rows.reshape(b, s, 4, NSA_KVH, HEAD_DIM))
            outs["win_p"].append(r3(win)[:, s - min(WINDOW, s):].reshape(b, min(WINDOW, s), 2, NSA_KVH, HEAD_DIM))
            outs["fox_p"].append(fkv.reshape(b, s, 2, FOX_H, HEAD_DIM))
            outs["lf_p"].append(logf)
            q, rows, selkv, win, winkv, small, nz, fq, fkv, fk, fv, fz = _even_in(xs, g, w, bfp, bd)
            new_rows = jnp.stack([rows[:, 2 * LANES:3 * LANES], rows[:, 3 * LANES:], win[:, :LANES], win[:, LANES:]],
                                 axis=1)
            gates = small[:, :SMALL_GATE].reshape(bd, NSA_H, 3)
            o8 = _nsa_decode(page_table, _group_rows(q), cache_nsa4, win4, new_rows, gates, tab_t, expand, pe, w1bd,
                             w2bd, li)
            o_n = o8[:, :, :HEAD_DIM].reshape(bd, NSA_W)
            logf = small[:, SMALL_GATE:SMALL_GATE + FOX_H]
            o_f = _fox_decode(page_table, _head_rows(fq, FOX_H), cache_fox4, lft, fkv.reshape(bd, 1, 2 * FOX_W),
                              logf.reshape(bd, FOX_H, 1), li).reshape(bd, FOX_W)
            xs = _out_proj(xs, [(o_n, nz), (o_f, fz)], w_out, gfin, final, bd)
            outs["nsa_s"].append(rows.reshape(bd, 1, 4, NSA_KVH, HEAD_DIM))
            outs["win_s"].append(jnp.concatenate([cache_nsa_win[:, li, 1:], win.reshape(bd, 1, 2, NSA_KVH, HEAD_DIM)],
                                                 axis=1))
            outs["fox_s"].append(fkv.reshape(bd, 1, 2, FOX_H, HEAD_DIM))
            outs["lf_s"].append(logf.reshape(bd, 1, FOX_H))
        else:
            w = w_in_odd[li].astype(BF16)
            w_out = w_out_odd[li].astype(BF16)
            q, kv, k, v, z = _odd_in(xp, g, w, tm_p)
            r3 = lambda a: a.reshape(b, s, a.shape[-1])
            o = _sb_prompt(r3(q), r3(k), r3(v), t)
            xp = _out_proj(xp, [(o.reshape(b * s, SB_W), z)], w_out, gfin, final, tm_o)
            outs["sb_p"].append(kv.reshape(b, s, 2, SB_H, HEAD_DIM))
            q, kv, k, v, z = _odd_in(xs, g, w, bd)
            o = _sb_decode(page_table, _head_rows(q, SB_H), cache_sb4, li).reshape(bd, SB_W)
            xs = _out_proj(xs, [(o, z)], w_out, gfin, final, bd)
            outs["sb_s"].append(kv.reshape(bd, 1, 2, SB_H, HEAD_DIM))

    st = lambda k: jnp.stack(outs[k], 1)
    return (xp.reshape(b, s, d), xs.reshape(bd, 1, d), st("nsa_p"), st("nsa_s"), st("win_p"), st("win_s"),
            st("fox_p"), st("fox_s"), st("lf_p"), st("lf_s"), st("sb_p"), st("sb_s"))
```

```python
import functools
import math

import numpy as np
import jax
import jax.numpy as jnp
from jax import lax
from jax.experimental import pallas as pl
from jax.experimental.pallas import tpu as pltpu

F32 = jnp.float32
BF16 = jnp.bfloat16

D_MODEL = 1024
HEAD_DIM = 64
LANES = 128
NSA_H = 8
NSA_KVH = 2
NSA_GROUP = NSA_H // NSA_KVH
FOX_H = 8
SB_H = 16
NSA_W = NSA_H * HEAD_DIM
FOX_W = FOX_H * HEAD_DIM
SB_W = SB_H * HEAD_DIM
CMP_BLOCK = 64
TOP_N = 16
WINDOW = 512
PAGE_SIZE = 128
N_BUCKETS = 32
REL_MAX_DIST = 128
RMS_EPS = 1e-6
NEG_INF = -1e30
SCALE = HEAD_DIM ** -0.5
EXP_ZERO = 104.0

ATT_T = 256
DECODE_PAGES = 8
VMEM_LIMIT = 56 * 1024 * 1024


def _cparams(*sem):
    return pltpu.CompilerParams(dimension_semantics=sem, vmem_limit_bytes=VMEM_LIMIT)


def _dot(a, b):
    return jnp.dot(a, b, preferred_element_type=F32)


def _dot_nt(a, b):
    return lax.dot_general(a, b, (((1,), (1,)), ((), ())), preferred_element_type=F32)


def _split2(x):
    hi = x.astype(BF16)
    lo = (x - hi.astype(F32)).astype(BF16)
    return hi, lo


def _dot_exact01(x, w01, passes, left=False):
    acc = None
    r = x
    for _ in range(passes):
        piece = r.astype(BF16)
        term = _dot(w01, piece) if left else _dot(piece, w01)
        acc = term if acc is None else acc + term
        r = r - piece.astype(F32)
    return acc


def _softplus(z):
    return jnp.maximum(z, 0.0) + jnp.log1p(jnp.exp(-jnp.abs(z)))


def _sigmoid(z):
    return 1.0 / (1.0 + jnp.exp(-z))


def _rms(x, g):
    return x * lax.rsqrt(jnp.mean(x * x, axis=-1, keepdims=True) + RMS_EPS) * g


def _lane_lo(shape):
    return lax.broadcasted_iota(jnp.int32, shape, len(shape) - 1) < HEAD_DIM


def _dup_halves(x):
    r = pltpu.roll(x, HEAD_DIM, 1)
    lo = _lane_lo(x.shape)
    return jnp.where(lo, x, r), jnp.where(lo, r, x)


E_Q, E_ROWS, E_WIN, E_NZ, E_FQ, E_FK, E_FV, E_FZ, E_SMALL, E_END = (
    0, 512, 1024, 1280, 1792, 2304, 2816, 3328, 3840, 3968)
SMALL_GATE = 3 * NSA_H


def _even_in_kernel(x_ref, g_ref, w_ref, bf_ref, q_ref, rows_ref, selkv_ref, win_ref, winkv_ref, small_ref,
                    nz_ref, fq_ref, fkv_ref, fk_ref, fv_ref, fz_ref):
    xn = _rms(x_ref[...], g_ref[...]).astype(BF16)

    def mm(a, b):
        return _dot(xn, w_ref[:, a:b])

    q_ref[...] = (mm(E_Q, E_ROWS) * SCALE).astype(BF16)
    rows = mm(E_ROWS, E_WIN)
    rows_ref[...] = rows
    kd0, kd1 = _dup_halves(rows[:, 256:384])
    vd0, vd1 = _dup_halves(rows[:, 384:512])
    selkv_ref[...] = jnp.concatenate([kd0, kd1, vd0, vd1], axis=1).astype(BF16)
    win = mm(E_WIN, E_NZ)
    win_ref[...] = win
    kd0, kd1 = _dup_halves(win[:, 0:128])
    vd0, vd1 = _dup_halves(win[:, 128:256])
    winkv_ref[...] = jnp.concatenate([kd0, kd1, vd0, vd1], axis=1).astype(BF16)
    nz_ref[...] = mm(E_NZ, E_FQ)
    fq_ref[...] = (mm(E_FQ, E_FK) * SCALE).astype(BF16)
    fkv = mm(E_FK, E_FZ)
    fkv_ref[...] = fkv
    fk_ref[...] = fkv[:, :FOX_W].astype(BF16)
    fv_ref[...] = fkv[:, FOX_W:].astype(BF16)
    fz_ref[...] = mm(E_FZ, E_SMALL)
    small = mm(E_SMALL, E_END)
    lane = lax.broadcasted_iota(jnp.int32, small.shape, 1)
    is_f = (lane >= SMALL_GATE) & (lane < SMALL_GATE + FOX_H)
    small_ref[...] = jnp.where(is_f, -_softplus(-(small + bf_ref[...])), small)


def _even_in(x, g, w, bfp, tm):
    m = x.shape[0]
    row = lambda c: pl.BlockSpec((tm, c), lambda i: (i, 0))
    full = lambda a: pl.BlockSpec(a.shape, lambda i: (0, 0))
    outs = [(512, BF16), (512, F32), (512, BF16), (256, F32), (512, BF16), (128, F32),
            (512, F32), (512, BF16), (1024, F32), (512, BF16), (512, BF16), (512, F32)]
    return pl.pallas_call(
        _even_in_kernel,
        grid=(m // tm,),
        in_specs=[row(D_MODEL), full(g), full(w), full(bfp)],
        out_specs=[row(c) for c, _ in outs],
        out_shape=[jax.ShapeDtypeStruct((m, c), dt) for c, dt in outs],
        compiler_params=_cparams("parallel"),
        name="even_in",
    )(x, g, w, bfp)


def _odd_in_kernel(x_ref, g_ref, w_ref, q_ref, kv_ref, k_ref, v_ref, z_ref):
    xn = _rms(x_ref[...], g_ref[...]).astype(BF16)
    q_ref[...] = (_dot(xn, w_ref[:, 0:SB_W]) * SCALE).astype(BF16)
    kv = _dot(xn, w_ref[:, SB_W:3 * SB_W])
    kv_ref[...] = kv
    k_ref[...] = kv[:, :SB_W].astype(BF16)
    v_ref[...] = kv[:, SB_W:].astype(BF16)
    z_ref[...] = _dot(xn, w_ref[:, 3 * SB_W:])


def _odd_in(x, g, w, tm):
    m = x.shape[0]
    row = lambda c: pl.BlockSpec((tm, c), lambda i: (i, 0))
    full = lambda a: pl.BlockSpec(a.shape, lambda i: (0, 0))
    outs = [(SB_W, BF16), (2 * SB_W, F32), (SB_W, BF16), (SB_W, BF16), (SB_W, F32)]
    return pl.pallas_call(
        _odd_in_kernel,
        grid=(m // tm,),
        in_specs=[row(D_MODEL), full(g), full(w)],
        out_specs=[row(c) for c, _ in outs],
        out_shape=[jax.ShapeDtypeStruct((m, c), dt) for c, dt in outs],
        compiler_params=_cparams("parallel"),
        name="odd_in",
    )(x, g, w)


def _out_kernel(n_parts, final, *refs):
    x_ref = refs[0]
    parts = refs[1:1 + 2 * n_parts]
    w_ref = refs[1 + 2 * n_parts]
    gf_ref = refs[2 + 2 * n_parts]
    y_ref = refs[3 + 2 * n_parts]
    y = x_ref[...]
    off = 0
    for p in range(n_parts):
        o = parts[2 * p][...]
        z = parts[2 * p + 1][...]
        k = o.shape[1]
        y = y + _dot((o * (z * _sigmoid(z))).astype(BF16), w_ref[off:off + k, :])
        off += k
    if final:
        y = _rms(y, gf_ref[...])
    y_ref[...] = y


def _out_proj(x, parts, w, gf, final, tm):
    m = x.shape[0]
    row = lambda c: pl.BlockSpec((tm, c), lambda i: (i, 0))
    full = lambda a: pl.BlockSpec(a.shape, lambda i: (0, 0))
    flat = [a for pr in parts for a in pr]
    return pl.pallas_call(
        functools.partial(_out_kernel, len(parts), final),
        grid=(m // tm,),
        in_specs=[row(D_MODEL)] + [row(a.shape[1]) for a in flat] + [full(w), full(gf)],
        out_specs=row(D_MODEL),
        out_shape=jax.ShapeDtypeStruct((m, D_MODEL), F32),
        compiler_params=_cparams("parallel"),
        name="out_proj",
    )(x, *flat, w, gf)


def _cumsum_kernel(x_ref, o_ref, *, chunk):
    n = x_ref.shape[2] // chunk
    r = lax.broadcasted_iota(jnp.int32, (chunk, chunk), 0)
    c = lax.broadcasted_iota(jnp.int32, (chunk, chunk), 1)
    tri = jnp.where(r <= c, 1.0, 0.0).astype(BF16)

    def body(j, carry):
        x = x_ref[0, :, pl.ds(j * chunk, chunk)]
        cs = _dot_exact01(x, tri, 3) + carry
        o_ref[0, :, pl.ds(j * chunk, chunk)] = cs
        return cs[:, chunk - 1:chunk]

    lax.fori_loop(0, n, body, jnp.zeros((x_ref.shape[1], 1), F32))


def _cumsum_lanes(x):
    b, h, s = x.shape
    spec = pl.BlockSpec((1, h, s), lambda i: (i, 0, 0))
    return pl.pallas_call(
        functools.partial(_cumsum_kernel, chunk=min(256, s)),
        grid=(b,),
        in_specs=[spec],
        out_specs=spec,
        out_shape=jax.ShapeDtypeStruct(x.shape, F32),
        compiler_params=_cparams("parallel"),
        name="logf_cumsum",
    )(x)


def _fox_prompt_kernel(q_ref, k_ref, v_ref, c_ref, o_ref, kmax_sc, *, t):
    i = pl.program_id(2)
    q2 = q_ref[0]
    lo = _lane_lo(q2.shape)
    zero = jnp.zeros_like(q2)
    qs = (jnp.where(lo, q2, zero), jnp.where(lo, zero, q2))
    row = lax.broadcasted_iota(jnp.int32, (t, t), 0)
    col = lax.broadcasted_iota(jnp.int32, (t, t), 1)
    causal = row >= col

    @pl.when(i == 0)
    def _():
        def norm_step(cidx, mx):
            kk = k_ref[0, pl.ds(cidx * t, t), :].astype(F32)
            kk = kk * kk
            n0 = jnp.max(jnp.sum(jnp.where(lo, kk, 0.0), axis=1, keepdims=True), axis=0, keepdims=True)
            n1 = jnp.max(jnp.sum(jnp.where(lo, 0.0, kk), axis=1, keepdims=True), axis=0, keepdims=True)
            return jnp.maximum(mx[0], n0), jnp.maximum(mx[1], n1)

        z11 = jnp.zeros((1, 1), F32)
        mx = lax.fori_loop(0, k_ref.shape[1] // t, norm_step, (z11, z11))
        for h in range(2):
            kmax_sc[h] = jnp.broadcast_to(jnp.sqrt(mx[h]), kmax_sc.shape[1:])

    qf = q2.astype(F32)
    qq = qf * qf
    bound = []
    for h in range(2):
        qn = jnp.sqrt(jnp.sum(jnp.where(lo, qq, 0.0) if h == 0 else jnp.where(lo, 0.0, qq), axis=1, keepdims=True))
        bound.append(qn * kmax_sc[h][0:1, 0:1] + 1.0)

    def step(j, carry, masked):
        kblk = k_ref[0, pl.ds(j * t, t), :]
        vblk = v_ref[0, pl.ds(j * t, t), :]
        cb = c_ref[0, 0, :, pl.ds(j * t, t)]
        new = []
        for h in range(2):
            m_old, l_old, acc = carry[h]
            s = _dot_nt(qs[h], kblk) - cb[h:h + 1, :]
            if masked:
                s = jnp.where(causal, s, NEG_INF)
            m_new = jnp.maximum(m_old, jnp.max(s, axis=1, keepdims=True))
            alpha = jnp.exp(m_old - m_new)
            p = jnp.exp(s - m_new)
            if masked:
                p = jnp.where(causal, p, 0.0)
            l_new = alpha * l_old + jnp.sum(p, axis=1, keepdims=True)
            acc = alpha * acc + _dot(p.astype(BF16), vblk)
            new.append((m_new, l_new, acc))
        return tuple(new)

    def older_tiles_matter(jj, carry):
        cend = c_ref[0, 0, :, pl.ds(jnp.maximum(jj, 0) * t, t)][:, t - 1:t]
        gap = [jnp.max(bound[h] - carry[h][0], axis=0, keepdims=True) - cend[h:h + 1, :] for h in range(2)]
        return (jnp.max(jnp.maximum(gap[0], gap[1])) >= -EXP_ZERO).astype(jnp.int32)

    def body(state):
        jj, _, carry = state
        carry = step(jj, carry, False)
        return jj - 1, older_tiles_matter(jj - 1, carry), carry

    init = tuple((jnp.full((t, 1), NEG_INF, F32), jnp.zeros((t, 1), F32), jnp.zeros((t, LANES), F32))
                 for _ in range(2))
    carry = step(i, init, True)
    _, _, carry = lax.while_loop(lambda st: (st[0] >= 0) & (st[1] > 0), body,
                                 (i - 1, older_tiles_matter(i - 1, carry), carry))
    outs = [acc / jnp.maximum(l, 1e-30) for (_, l, acc) in carry]
    o_ref[0] = jnp.where(lo, outs[0], outs[1])


def _fox_prompt(q, k, v, c4, t):
    b, s, w = q.shape
    hp = w // LANES
    return pl.pallas_call(
        functools.partial(_fox_prompt_kernel, t=t),
        grid=(b, hp, s // t),
        in_specs=[pl.BlockSpec((1, t, LANES), lambda bi, h, i: (bi, i, h)),
                  pl.BlockSpec((1, s, LANES), lambda bi, h, i: (bi, 0, h)),
                  pl.BlockSpec((1, s, LANES), lambda bi, h, i: (bi, 0, h)),
                  pl.BlockSpec((1, 1, 2, s), lambda bi, h, i: (bi, h, 0, 0))],
        out_specs=pl.BlockSpec((1, t, LANES), lambda bi, h, i: (bi, i, h)),
        out_shape=jax.ShapeDtypeStruct((b, s, w), F32),
        scratch_shapes=[pltpu.VMEM((2, 8, LANES), F32)],
        compiler_params=_cparams("parallel", "parallel", "arbitrary"),
        name="fox_prompt",
    )(q, k, v, c4)


def _sb_weights(z, after_carry, ustrict, mask):
    sp = _softplus(z)
    l1m = -sp if mask is None else jnp.where(mask, -sp, 0.0)
    after = _dot_exact01(l1m, ustrict, 2) + after_carry
    a = jnp.exp(z - sp + after)
    if mask is not None:
        a = jnp.where(mask, a, 0.0)
    return a, after_carry + jnp.sum(l1m, axis=1, keepdims=True)


def _sb_prompt_kernel(q_ref, k_ref, v_ref, o_ref, *, t):
    i = pl.program_id(2)
    q2 = q_ref[0]
    lo = _lane_lo(q2.shape)
    zero = jnp.zeros_like(q2)
    qs = (jnp.where(lo, q2, zero), jnp.where(lo, zero, q2))
    row = lax.broadcasted_iota(jnp.int32, (t, t), 0)
    col = lax.broadcasted_iota(jnp.int32, (t, t), 1)
    strict = row > col
    ustrict = jnp.where(strict, 1.0, 0.0).astype(BF16)

    def step(j, carry, mask):
        kblk = k_ref[0, pl.ds(j * t, t), :]
        vblk = v_ref[0, pl.ds(j * t, t), :]
        new = []
        for h in range(2):
            run, acc = carry[h]
            a, run = _sb_weights(_dot_nt(qs[h], kblk), run, ustrict, mask)
            new.append((run, acc + _dot(a.astype(BF16), vblk)))
        return tuple(new)

    def older_tiles_matter(carry):
        return (jnp.max(jnp.maximum(carry[0][0], carry[1][0])) >= -EXP_ZERO).astype(jnp.int32)

    def body(state):
        jj, _, carry = state
        carry = step(jj, carry, None)
        return jj - 1, older_tiles_matter(carry), carry

    init = tuple((jnp.zeros((t, 1), F32), jnp.zeros((t, LANES), F32)) for _ in range(2))
    carry = step(i, init, strict)
    _, _, carry = lax.while_loop(lambda st: (st[0] >= 0) & (st[1] > 0), body,
                                 (i - 1, older_tiles_matter(carry), carry))
    o_ref[0] = jnp.where(lo, carry[0][1], carry[1][1])


def _sb_prompt(q, k, v, t):
    b, s, w = q.shape
    hp = w // LANES
    return pl.pallas_call(
        functools.partial(_sb_prompt_kernel, t=t),
        grid=(b, hp, s // t),
        in_specs=[pl.BlockSpec((1, t, LANES), lambda bi, h, i: (bi, i, h)),
                  pl.BlockSpec((1, s, LANES), lambda bi, h, i: (bi, 0, h)),
                  pl.BlockSpec((1, s, LANES), lambda bi, h, i: (bi, 0, h))],
        out_specs=pl.BlockSpec((1, t, LANES), lambda bi, h, i: (bi, i, h)),
        out_shape=jax.ShapeDtypeStruct((b, s, w), F32),
        compiler_params=_cparams("parallel", "parallel", "arbitrary"),
        name="sb_prompt",
    )(q, k, v)


def _bucket_starts():
    d = np.arange(0, 4 * REL_MAX_DIST)
    max_exact = N_BUCKETS // 2
    large = max_exact + (np.log(np.maximum(d, 1).astype(np.float32) / max_exact)
                         / math.log(REL_MAX_DIST / max_exact) * (N_BUCKETS - max_exact)).astype(np.int32)
    b = np.where(d < max_exact, d, np.minimum(large, N_BUCKETS - 1))
    return [int(np.min(np.nonzero(b >= k)[0])) for k in range(N_BUCKETS)]


BUCKET_STARTS = _bucket_starts()
BIAS_FAR_DIST = BUCKET_STARTS[-1]


def _bias_from_dist(dist, tab):
    v = jnp.where(dist >= BUCKET_STARTS[1], tab(1), tab(0))
    for b in range(2, N_BUCKETS):
        v = jnp.where(dist >= BUCKET_STARTS[b], tab(b), v)
    return v


def _masked_softmax(s, mask):
    logits = jnp.where(mask, s, NEG_INF)
    m = jnp.max(logits, axis=-1, keepdims=True)
    e = jnp.where(mask, jnp.exp(logits - m), 0.0)
    return e / jnp.maximum(jnp.sum(e, axis=-1, keepdims=True), 1e-30)


def _top_blocks(score, n_top):
    lanef = lax.broadcasted_iota(jnp.int32, score.shape, 1).astype(F32)
    work = score
    msel = jnp.zeros(score.shape, F32)
    for _ in range(n_top):
        mx = jnp.max(work, axis=1, keepdims=True)
        idx = jnp.min(jnp.where(work == mx, lanef, 1e9), axis=1, keepdims=True)
        pick = lanef == idx
        msel = jnp.where(pick, 1.0, msel)
        work = jnp.where(pick, -2.0, work)
    return msel


def _compress_rows(k_ref, v_ref, nb, pe_ref, w1_ref, w2_ref):
    outs = []
    for kv, ref in enumerate((k_ref, v_ref)):
        acc = jnp.zeros((nb, LANES), F32)
        for pos in range(CMP_BLOCK):
            x = ref[pl.ds(pos, nb, stride=CMP_BLOCK), :] + pe_ref[kv, pos:pos + 1, :]
            acc = acc + _dot(x.astype(BF16), w1_ref[kv, pos])
        hid = acc * _sigmoid(acc)
        outs.append(_dot(hid.astype(BF16), w2_ref[kv]))
    return outs


def _compress_kernel(k_ref, v_ref, pe_ref, w1_ref, w2_ref, o_ref):
    nb = o_ref.shape[1]
    kc, vc = _compress_rows(k_ref.at[0], v_ref.at[0], nb, pe_ref, w1_ref, w2_ref)
    o_ref[0] = jnp.concatenate([kc, vc], axis=1).astype(BF16)


def _compress_prompt(rows, pe, w1bd, w2big):
    b, s, _ = rows.shape
    nb = s // CMP_BLOCK
    full = lambda a: pl.BlockSpec(a.shape, lambda i: (0,) * a.ndim)
    return pl.pallas_call(
        _compress_kernel,
        grid=(b,),
        in_specs=[pl.BlockSpec((1, s, LANES), lambda i: (i, 0, 0)), pl.BlockSpec((1, s, LANES), lambda i: (i, 0, 1)),
                  full(pe), full(w1bd), full(w2big)],
        out_specs=pl.BlockSpec((1, nb, 4 * LANES), lambda i: (i, 0, 0)),
        out_shape=jax.ShapeDtypeStruct((b, nb, 4 * LANES), BF16),
        compiler_params=_cparams("parallel"),
        name="nsa_compress_prompt",
    )(rows, rows, pe, w1bd, w2big)


def _nsa_prompt_kernel(tab_ref, q_ref, ks_ref, vs_ref, kw_ref, vw_ref, kc_ref, vc_ref, gate_ref, o_ref, tiles_sc,
                       *, t, n_top):
    g = pl.program_id(1)
    i = pl.program_id(2)
    nh = NSA_GROUP
    nb = kc_ref.shape[1]
    row = lax.broadcasted_iota(jnp.int32, (t, t), 0)
    col = lax.broadcasted_iota(jnp.int32, (t, t), 1)

    @pl.when(i == 0)
    def _():
        for hh in range(nh):
            h = g * nh + hh
            for kind in range(2):
                tiles_sc[kind, hh] = _bias_from_dist(kind * t + row - col, lambda b: tab_ref[b, h])
            tiles_sc[2, hh] = jnp.full((t, t), tab_ref[N_BUCKETS - 1, h], F32)

    q4 = q_ref[0]
    lo = _lane_lo((t, LANES))
    zero = jnp.zeros((t, LANES), BF16)
    qa, qb = q4[:, :LANES], q4[:, LANES:]
    qst = jnp.concatenate([jnp.where(lo, qa, zero), jnp.where(lo, zero, qa),
                           jnp.where(lo, qb, zero), jnp.where(lo, zero, qb)], axis=0)

    rowc = lax.broadcasted_iota(jnp.int32, (t, nb), 0)
    colc = lax.broadcasted_iota(jnp.int32, (t, nb), 1)
    dist_c = (i * t + rowc) - (colc * CMP_BLOCK + (CMP_BLOCK - 1))
    bias_c = jnp.stack([_bias_from_dist(dist_c, functools.partial(lambda b, h: tab_ref[b, h], h=g * nh + hh))
                        for hh in range(nh)], axis=0)
    valid_c = (dist_c >= 0)[None]
    p_c = _masked_softmax(_dot_nt(qst, kc_ref[0]).reshape(nh, t, nb) + bias_c, valid_c)
    o_c = _dot(p_c.reshape(nh * t, nb).astype(BF16), vc_ref[0])

    cur = (i * t + rowc) // CMP_BLOCK
    forced = (colc == 0) | (colc == cur) | (colc == cur - 1)
    score = jnp.where(forced, NSA_GROUP + 1.0, jnp.where(colc <= cur, jnp.sum(p_c, axis=0), -1.0))
    msel = _top_blocks(score, n_top).astype(BF16)

    blk_of_key = lax.broadcasted_iota(jnp.int32, (nb, t), 0)
    key_in_tile = lax.broadcasted_iota(jnp.int32, (nb, t), 1)

    def flash_step(j, carry, k_ref, v_ref, selected):
        m_old, l_old, acc = carry
        kblk = k_ref[0, pl.ds(j * t, t), :]
        vblk = v_ref[0, pl.ds(j * t, t), :]
        dist = (i - j) * t + row - col
        if selected:
            expand = jnp.where(blk_of_key == jnp.right_shift(j * t + key_in_tile, 6), 1.0, 0.0).astype(BF16)
            mask = (_dot(msel, expand) > 0.5) & (dist >= 0)
        else:
            mask = (dist >= 0) & (dist <= WINDOW)
        mask = mask[None]
        s = _dot_nt(qst, kblk).reshape(nh, t, t) + tiles_sc[jnp.minimum(i - j, 2)]
        s = jnp.where(mask, s, NEG_INF)
        m_new = jnp.maximum(m_old, jnp.max(s, axis=-1, keepdims=True))
        alpha = jnp.exp(m_old - m_new)
        p = jnp.where(mask, jnp.exp(s - m_new), 0.0)
        l_new = alpha * l_old + jnp.sum(p, axis=-1, keepdims=True)
        acc = alpha.reshape(nh * t, 1) * acc + _dot(p.reshape(nh * t, t).astype(BF16), vblk)
        return m_new, l_new, acc

    def finish(carry):
        _, l, acc = carry
        return acc / jnp.maximum(l, 1e-30).reshape(nh * t, 1)

    init = (jnp.full((nh, t, 1), NEG_INF, F32), jnp.zeros((nh, t, 1), F32), jnp.zeros((nh * t, LANES), F32))
    o_s = finish(lax.fori_loop(0, i + 1, lambda j, c: flash_step(j, c, ks_ref, vs_ref, True), init))
    first_w = jnp.maximum(i - (WINDOW + t - 1) // t, 0)
    o_w = finish(lax.fori_loop(first_w, i + 1, lambda j, c: flash_step(j, c, kw_ref, vw_ref, False), init))

    sig = _sigmoid(gate_ref[0])
    lane = lax.broadcasted_iota(jnp.int32, sig.shape, 1)

    def gate(hh, br):
        return jnp.sum(jnp.where(lane == (g * nh + hh) * 3 + br, sig, 0.0), axis=1, keepdims=True)

    outs = []
    for hh in range(nh):
        sl = slice(hh * t, (hh + 1) * t)
        outs.append(gate(hh, 0) * o_c[sl] + gate(hh, 1) * o_s[sl] + gate(hh, 2) * o_w[sl])
    o_ref[0] = jnp.concatenate([jnp.where(lo, outs[0], outs[1]), jnp.where(lo, outs[2], outs[3])], axis=1)


def _nsa_prompt(tab, q, selkv, winkv, cmp, small, t):
    b, s, _ = q.shape
    nb = cmp.shape[1]
    assert t >= BIAS_FAR_DIST and s % t == 0
    kv = lambda off: pl.BlockSpec((1, s, LANES), lambda bi, g, i: (bi, 0, off + g))
    cm = lambda off: pl.BlockSpec((1, nb, LANES), lambda bi, g, i: (bi, 0, off + g))
    return pl.pallas_call(
        functools.partial(_nsa_prompt_kernel, t=t, n_top=min(TOP_N, nb)),
        grid=(b, NSA_KVH, s // t),
        in_specs=[pl.BlockSpec(memory_space=pltpu.SMEM),
                  pl.BlockSpec((1, t, 2 * LANES), lambda bi, g, i: (bi, i, g)),
                  kv(0), kv(2), kv(0), kv(2), cm(0), cm(2),
                  pl.BlockSpec((1, t, LANES), lambda bi, g, i: (bi, i, 0))],
        out_specs=pl.BlockSpec((1, t, 2 * LANES), lambda bi, g, i: (bi, i, g)),
        out_shape=jax.ShapeDtypeStruct((b, s, NSA_W), F32),
        scratch_shapes=[pltpu.VMEM((3, NSA_GROUP, t, t), F32)],
        compiler_params=_cparams("parallel", "parallel", "arbitrary"),
        name="nsa_prompt",
    )(tab, q, selkv, selkv, winkv, winkv, cmp, cmp, small)


def _head_diag(x, width):
    r = lax.broadcasted_iota(jnp.int32, x.shape, 0)
    c = lax.broadcasted_iota(jnp.int32, x.shape, 1)
    own = (c >= r * width) & (c < (r + 1) * width)
    return jnp.sum(jnp.where(own, x, 0.0), axis=0, keepdims=True)


def _page_specs(n, block, page_of, li):
    return [pl.BlockSpec(block, functools.partial(
        lambda b, p, pt, g: (pt[b, page_of(p, g)], li) + (0,) * (len(block) - 2), g=g)) for g in range(n)]


def _fox_decode_kernel(pt_ref, q_ref, *refs, n_pages):
    page_refs = refs[:n_pages]
    lf_refs = refs[n_pages:2 * n_pages]
    kvn_ref, lfn_ref, o_ref, m_sc, l_sc, acc_sc, c_sc = refs[2 * n_pages:]
    p = pl.program_id(1)
    w = FOX_W

    @pl.when(p == 0)
    def _():
        m_sc[...] = jnp.full(m_sc.shape, NEG_INF, F32)
        l_sc[...] = jnp.zeros(l_sc.shape, F32)
        acc_sc[...] = jnp.zeros(acc_sc.shape, F32)
        c_sc[...] = jnp.zeros(c_sc.shape, F32)

    q = q_ref[0]
    r = lax.broadcasted_iota(jnp.int32, (PAGE_SIZE, PAGE_SIZE), 0)
    cidx = lax.broadcasted_iota(jnp.int32, (PAGE_SIZE, PAGE_SIZE), 1)
    tri = jnp.where(r <= cidx, 1.0, 0.0).astype(BF16)
    m, l, acc, c_run = m_sc[...], l_sc[...], acc_sc[...], c_sc[...]
    for page_ref, lf_ref in zip(page_refs, lf_refs):
        c = _dot_exact01(lf_ref[0, 0], tri, 3) + c_run
        c_run = c[:, PAGE_SIZE - 1:PAGE_SIZE]
        s = _dot(q, page_ref[0, 0, 0].astype(BF16)) - c
        m_new = jnp.maximum(m, jnp.max(s, axis=1, keepdims=True))
        alpha = jnp.exp(m - m_new)
        e = jnp.exp(s - m_new)
        l = alpha * l + jnp.sum(e, axis=1, keepdims=True)
        acc = alpha * acc + _dot_nt(e.astype(BF16), page_ref[0, 0, 1].astype(BF16))
        m = m_new
    m_sc[...], l_sc[...], acc_sc[...], c_sc[...] = m, l, acc, c_run

    @pl.when(p == pl.num_programs(1) - 1)
    def _():
        kvn = kvn_ref[0].astype(BF16).astype(F32)
        s_n = jnp.sum(q.astype(F32) * kvn[:, :w], axis=1, keepdims=True) - (c_run + lfn_ref[0])
        m_fin = jnp.maximum(m, s_n)
        a = jnp.exp(m - m_fin)
        e_n = jnp.exp(s_n - m_fin)
        l_fin = a * l + e_n
        out = a * acc + e_n.astype(BF16).astype(F32) * kvn[:, w:]
        o_ref[0] = _head_diag(out / jnp.maximum(l_fin, 1e-30), HEAD_DIM)


def _fox_decode(page_table, qrows, cache_t, lft, kv_new, lf_new, li, n_pages):
    bd, npg = page_table.shape
    h = FOX_H
    assert npg % n_pages == 0
    page_of = lambda p, g: p * n_pages + g
    grid_spec = pltpu.PrefetchScalarGridSpec(
        num_scalar_prefetch=1,
        grid=(bd, npg // n_pages),
        in_specs=[pl.BlockSpec((1, h, FOX_W), lambda b, p, pt: (b, 0, 0))]
        + _page_specs(n_pages, (1, 1, 2, FOX_W, PAGE_SIZE), page_of, li)
        + _page_specs(n_pages, (1, 1, h, PAGE_SIZE), page_of, li)
        + [pl.BlockSpec((1, 1, 2 * FOX_W), lambda b, p, pt: (b, 0, 0)),
           pl.BlockSpec((1, h, 1), lambda b, p, pt: (b, 0, 0))],
        out_specs=pl.BlockSpec((1, 1, FOX_W), lambda b, p, pt: (b, 0, 0)),
        scratch_shapes=[pltpu.VMEM((h, 1), F32), pltpu.VMEM((h, 1), F32), pltpu.VMEM((h, FOX_W), F32),
                        pltpu.VMEM((h, 1), F32)])
    return pl.pallas_call(
        functools.partial(_fox_decode_kernel, n_pages=n_pages),
        grid_spec=grid_spec,
        out_shape=jax.ShapeDtypeStruct((bd, 1, FOX_W), F32),
        compiler_params=_cparams("parallel", "arbitrary"),
        name="fox_decode",
    )(page_table, qrows, *([cache_t] * n_pages), *([lft] * n_pages), kv_new, lf_new)


def _sb_decode_kernel(pt_ref, q_ref, *refs, n_pages):
    page_refs = refs[:n_pages]
    o_ref, run_sc, acc_sc = refs[n_pages:]
    p = pl.program_id(1)

    @pl.when(p == 0)
    def _():
        run_sc[...] = jnp.zeros(run_sc.shape, F32)
        acc_sc[...] = jnp.zeros(acc_sc.shape, F32)

    r = lax.broadcasted_iota(jnp.int32, (PAGE_SIZE, PAGE_SIZE), 0)
    cidx = lax.broadcasted_iota(jnp.int32, (PAGE_SIZE, PAGE_SIZE), 1)
    ustrict = jnp.where(r > cidx, 1.0, 0.0).astype(BF16)
    q = q_ref[0]
    run, acc = run_sc[...], acc_sc[...]
    for page_ref in page_refs:
        a, run = _sb_weights(_dot(q, page_ref[0, 0, 0].astype(BF16)), run, ustrict, None)
        acc = acc + _dot_nt(a.astype(BF16), page_ref[0, 0, 1].astype(BF16))
    run_sc[...], acc_sc[...] = run, acc

    @pl.when(p == pl.num_programs(1) - 1)
    def _():
        o_ref[0] = _head_diag(acc, HEAD_DIM)


def _sb_decode(page_table, qrows, cache_t, li, n_pages):
    bd, npg = page_table.shape
    assert npg % n_pages == 0
    page_of = lambda p, g: npg - 1 - (p * n_pages + g)
    grid_spec = pltpu.PrefetchScalarGridSpec(
        num_scalar_prefetch=1,
        grid=(bd, npg // n_pages),
        in_specs=[pl.BlockSpec((1, SB_H, SB_W), lambda b, p, pt: (b, 0, 0))]
        + _page_specs(n_pages, (1, 1, 2, SB_W, PAGE_SIZE), page_of, li),
        out_specs=pl.BlockSpec((1, 1, SB_W), lambda b, p, pt: (b, 0, 0)),
        scratch_shapes=[pltpu.VMEM((SB_H, 1), F32), pltpu.VMEM((SB_H, SB_W), F32)])
    return pl.pallas_call(
        functools.partial(_sb_decode_kernel, n_pages=n_pages),
        grid_spec=grid_spec,
        out_shape=jax.ShapeDtypeStruct((bd, 1, SB_W), F32),
        compiler_params=_cparams("parallel", "arbitrary"),
        name="sb_decode",
    )(page_table, qrows, *([cache_t] * n_pages))


def _nsa_decode_kernel(pt_ref, q_ref, *refs, n_pages, n_top):
    page_refs = refs[:n_pages]
    (win_ref, new_ref, gate_ref, tab_ref, expand_ref, pe_ref, w1_ref, w2_ref, o_ref,
     kc_sc, vc_sc, ks_sc, vs_sc) = refs[n_pages:]
    p = pl.program_id(1)
    past = kc_sc.shape[0]
    nb = past // CMP_BLOCK
    for g, page_ref in enumerate(page_refs):
        rows = pl.ds(pl.multiple_of((p * n_pages + g) * PAGE_SIZE, PAGE_SIZE), PAGE_SIZE)
        kc_sc[rows, :] = page_ref[0, 0, 0].T
        vc_sc[rows, :] = page_ref[0, 0, 1].T
        ks_sc[:, rows] = page_ref[0, 0, 2].astype(BF16)
        vs_sc[:, rows] = page_ref[0, 0, 3].astype(BF16)

    @pl.when(p == pl.num_programs(1) - 1)
    def _():
        q = q_ref[0]
        qf = q.astype(F32)
        tab = lambda b: tab_ref[:, b:b + 1]
        new = new_ref[0].astype(BF16).astype(F32)

        def with_new(s, mask, k_new, v_new, v_past_t):
            s_n = jnp.sum(qf * k_new, axis=1, keepdims=True) + tab(0)
            s = jnp.where(mask, s, NEG_INF)
            m = jnp.maximum(jnp.max(s, axis=1, keepdims=True), s_n)
            e = jnp.where(mask, jnp.exp(s - m), 0.0)
            e_n = jnp.exp(s_n - m)
            l = jnp.sum(e, axis=1, keepdims=True) + e_n
            acc = _dot_nt(e.astype(BF16), v_past_t) + e_n.astype(BF16).astype(F32) * v_new
            return acc / jnp.maximum(l, 1e-30)

        kc, vc = _compress_rows(kc_sc, vc_sc, nb, pe_ref, w1_ref, w2_ref)
        colc = lax.broadcasted_iota(jnp.int32, (NSA_H, nb), 1)
        dist_c = past - (colc * CMP_BLOCK + (CMP_BLOCK - 1))
        s_c = _dot_nt(q, kc.astype(BF16)) + _bias_from_dist(dist_c, tab)
        p_c = _masked_softmax(s_c, dist_c >= 0)
        o_c = _dot(p_c.astype(BF16), vc.astype(BF16))

        rg = lax.broadcasted_iota(jnp.int32, (NSA_H, NSA_H), 0) // NSA_GROUP
        cg = lax.broadcasted_iota(jnp.int32, (NSA_H, NSA_H), 1) // NSA_GROUP
        same_group = jnp.where(rg == cg, 1.0, 0.0).astype(BF16)
        score = _dot_exact01(p_c, same_group, 3, left=True)
        cur = past // CMP_BLOCK
        forced = (colc == 0) | (colc == cur) | (colc == cur - 1)
        score = jnp.where(forced, NSA_GROUP + 1.0, jnp.where(colc <= cur, score, -1.0))
        msel = _top_blocks(score, n_top - 1).astype(BF16)

        cols = lax.broadcasted_iota(jnp.int32, (NSA_H, past), 1)
        dist_s = past - cols
        s_s = _dot(q, ks_sc[...]) + _bias_from_dist(dist_s, tab)
        o_s = with_new(s_s, _dot(msel, expand_ref[...]) > 0.5, new[0:1], new[1:2], vs_sc[...])

        wb = win_ref.shape[-1]
        colw = lax.broadcasted_iota(jnp.int32, (NSA_H, wb), 1)
        dist_w = wb - colw
        s_w = _dot(q, win_ref[0, 0, 0].astype(BF16)) + _bias_from_dist(dist_w, tab)
        o_w = with_new(s_w, dist_w <= WINDOW, new[2:3], new[3:4], win_ref[0, 0, 1].astype(BF16))

        gt = _sigmoid(gate_ref[0])
        o = gt[:, 0:1] * o_c + gt[:, 1:2] * o_s + gt[:, 2:3] * o_w
        first_group = lax.broadcasted_iota(jnp.int32, o.shape, 0) < NSA_GROUP
        o_ref[0] = jnp.where(first_group, o, pltpu.roll(o, HEAD_DIM, 1))


def _nsa_decode(page_table, qg, cache_t, win_t, new_rows, gates, tab_t, expand, pe, w1bd, w2bd, li, n_pages):
    bd, npg = page_table.shape
    past = npg * PAGE_SIZE
    nb = past // CMP_BLOCK
    wb = win_t.shape[-1]
    assert npg % n_pages == 0
    full = lambda a: pl.BlockSpec(a.shape, lambda b, p, pt: (0,) * a.ndim)
    per_seq = lambda a: pl.BlockSpec((1,) + a.shape[1:], lambda b, p, pt: (b,) + (0,) * (a.ndim - 1))
    grid_spec = pltpu.PrefetchScalarGridSpec(
        num_scalar_prefetch=1,
        grid=(bd, npg // n_pages),
        in_specs=[per_seq(qg)]
        + _page_specs(n_pages, (1, 1, 4, LANES, PAGE_SIZE), lambda p, g: p * n_pages + g, li)
        + [pl.BlockSpec((1, 1, 2, LANES, wb), lambda b, p, pt: (b, li, 0, 0, 0)),
           per_seq(new_rows), per_seq(gates), full(tab_t), full(expand), full(pe), full(w1bd), full(w2bd)],
        out_specs=pl.BlockSpec((1, NSA_H, LANES), lambda b, p, pt: (b, 0, 0)),
        scratch_shapes=[pltpu.VMEM((past, LANES), F32), pltpu.VMEM((past, LANES), F32),
                        pltpu.VMEM((LANES, past), BF16), pltpu.VMEM((LANES, past), BF16)])
    return pl.pallas_call(
        functools.partial(_nsa_decode_kernel, n_pages=n_pages, n_top=min(TOP_N, nb + 1)),
        grid_spec=grid_spec,
        out_shape=jax.ShapeDtypeStruct((bd, NSA_H, LANES), F32),
        compiler_params=_cparams("parallel", "arbitrary"),
        name="nsa_decode",
    )(page_table, qg, *([cache_t] * n_pages), win_t, new_rows, gates, tab_t, expand, pe, w1bd, w2bd)


def _prep_even_weights(w_in, b_f, cmp_pos, w1, w2):
    w = jnp.concatenate([w_in[:, 0:1280], w_in[:, 1304:1816], w_in[:, 1816:3352], w_in[:, 3360:3872],
                         w_in[:, 1280:1304], w_in[:, 3352:3360],
                         jnp.zeros((D_MODEL, LANES - SMALL_GATE - FOX_H), w_in.dtype)], axis=1).astype(BF16)
    bfp = jnp.zeros((1, LANES), F32).at[0, SMALL_GATE:SMALL_GATE + FOX_H].set(b_f)
    w1r = w1.reshape(2, CMP_BLOCK, HEAD_DIM, HEAD_DIM)
    z1 = jnp.zeros_like(w1r)
    w1bd = jnp.concatenate([jnp.concatenate([w1r, z1], -1), jnp.concatenate([z1, w1r], -1)], -2).astype(BF16)
    pe = jnp.concatenate([cmp_pos, cmp_pos], -1)
    z2 = jnp.zeros_like(w2)
    w2dup = jnp.concatenate([jnp.concatenate([w2, w2, z2, z2], -1), jnp.concatenate([z2, z2, w2, w2], -1)], -2).astype(BF16)
    w2bd = jnp.concatenate([jnp.concatenate([w2, z2], -1), jnp.concatenate([z2, w2], -1)], -2).astype(BF16)
    return w, bfp, pe, w1bd, w2dup, w2bd


def _head_rows(q, n_heads):
    bd = q.shape[0]
    q3 = q.reshape(bd, 1, n_heads, HEAD_DIM)
    eye = jnp.eye(n_heads, dtype=q.dtype).reshape(1, n_heads, n_heads, 1)
    return (q3 * eye).reshape(bd, n_heads, n_heads * HEAD_DIM)


def _group_rows(q):
    bd = q.shape[0]
    q3 = q.reshape(bd, NSA_H, HEAD_DIM)
    z = jnp.zeros_like(q3)
    first = (jnp.arange(NSA_H) < NSA_GROUP).reshape(1, NSA_H, 1)
    return jnp.where(first, jnp.concatenate([q3, z], -1), jnp.concatenate([z, q3], -1))


def kernel(x_prompt, x_sample, cache_nsa, cache_nsa_win, cache_fox, cache_fox_logf, cache_sb, page_table, norm_g, final_g, rel_bias, w_in_even, w_out_even, b_forget, cmp_pos, w_cmp1, w_cmp2, w_in_odd, w_out_odd):
    b, s, d = x_prompt.shape
    bd, ds_, _ = x_sample.shape
    assert ds_ == 1 and d == D_MODEL
    depth = norm_g.shape[0]
    n_even = w_in_even.shape[0]
    n_odd = w_in_odd.shape[0]
    n_phys = cache_nsa.shape[0]
    npg = page_table.shape[1]
    past = npg * PAGE_SIZE
    wb = cache_nsa_win.shape[2]
    t = min(ATT_T, s)
    tm_p = 256
    tm_o = 512

    xp = x_prompt.reshape(b * s, d)
    xs = x_sample.reshape(bd, d)
    gfin = final_g.reshape(1, d)
    rows_last = lambda a: jnp.transpose(a, (0, 1, 3, 4, 5, 2))
    cache_nsa_t = rows_last(cache_nsa).reshape(n_phys, n_even, 4, LANES, PAGE_SIZE)
    win_t = rows_last(cache_nsa_win).reshape(bd, n_even, 2, LANES, wb)
    cache_fox_t = rows_last(cache_fox).reshape(n_phys, n_even, 2, FOX_W, PAGE_SIZE)
    lft = jnp.transpose(cache_fox_logf.astype(F32), (0, 1, 3, 2))
    cache_sb_t = rows_last(cache_sb).reshape(n_phys, n_odd, 2, SB_W, PAGE_SIZE)
    n_pages = math.gcd(npg, DECODE_PAGES)
    tab_t = rel_bias.T
    expand = (jnp.arange(past)[None, :] // CMP_BLOCK == jnp.arange(past // CMP_BLOCK)[:, None]).astype(BF16)

    outs = {k: [] for k in ("nsa_p", "nsa_s", "win_p", "win_s", "fox_p", "fox_s", "lf_p", "lf_s", "sb_p", "sb_s")}
    for layer in range(depth):
        g = norm_g[layer].reshape(1, d)
        li = layer // 2
        final = layer == depth - 1
        if layer % 2 == 0:
            w, bfp, pe, w1bd, w2dup, w2bd = _prep_even_weights(w_in_even[li], b_forget[li], cmp_pos[li], w_cmp1[li],
                                                               w_cmp2[li])
            w_out = w_out_even[li].astype(BF16)
            q, rows, selkv, win, winkv, small, nz, fq, fkv, fk, fv, fz = _even_in(xp, g, w, bfp, tm_p)
            r3 = lambda a: a.reshape(b, s, a.shape[-1])
            cmp = _compress_prompt(r3(rows), pe, w1bd, w2dup)
            o_n = _nsa_prompt(rel_bias, r3(q), r3(selkv), r3(winkv), cmp, r3(small), t)
            logf = r3(small)[:, :, SMALL_GATE:SMALL_GATE + FOX_H]
            c = _cumsum_lanes(jnp.transpose(logf, (0, 2, 1)))
            o_f = _fox_prompt(r3(fq), r3(fk), r3(fv), c.reshape(b, FOX_H // 2, 2, s), t)
            xp = _out_proj(xp, [(o_n.reshape(b * s, NSA_W), nz), (o_f.reshape(b * s, FOX_W), fz)], w_out, gfin, final,
                           tm_o)
            outs["nsa_p"].append(rows.reshape(b, s, 4, NSA_KVH, HEAD_DIM))
            outs["win_p"].append(r3(win)[:, s - min(WINDOW, s):].reshape(b, min(WINDOW, s), 2, NSA_KVH, HEAD_DIM))
            outs["fox_p"].append(fkv.reshape(b, s, 2, FOX_H, HEAD_DIM))
            outs["lf_p"].append(logf)
            q, rows, selkv, win, winkv, small, nz, fq, fkv, fk, fv, fz = _even_in(xs, g, w, bfp, bd)
            new_rows = jnp.stack([rows[:, 2 * LANES:3 * LANES], rows[:, 3 * LANES:], win[:, :LANES], win[:, LANES:]],
                                 axis=1)
            gates = small[:, :SMALL_GATE].reshape(bd, NSA_H, 3)
            o8 = _nsa_decode(page_table, _group_rows(q), cache_nsa_t, win_t, new_rows, gates, tab_t, expand, pe, w1bd,
                             w2bd, li, n_pages)
            o_n = o8[:, :, :HEAD_DIM].reshape(bd, NSA_W)
            logf = small[:, SMALL_GATE:SMALL_GATE + FOX_H]
            o_f = _fox_decode(page_table, _head_rows(fq, FOX_H), cache_fox_t, lft, fkv.reshape(bd, 1, 2 * FOX_W),
                              logf.reshape(bd, FOX_H, 1), li, n_pages).reshape(bd, FOX_W)
            xs = _out_proj(xs, [(o_n, nz), (o_f, fz)], w_out, gfin, final, bd)
            outs["nsa_s"].append(rows.reshape(bd, 1, 4, NSA_KVH, HEAD_DIM))
            outs["win_s"].append(jnp.concatenate([cache_nsa_win[:, li, 1:], win.reshape(bd, 1, 2, NSA_KVH, HEAD_DIM)],
                                                 axis=1))
            outs["fox_s"].append(fkv.reshape(bd, 1, 2, FOX_H, HEAD_DIM))
            outs["lf_s"].append(logf.reshape(bd, 1, FOX_H))
        else:
            w = w_in_odd[li].astype(BF16)
            w_out = w_out_odd[li].astype(BF16)
            q, kv, k, v, z = _odd_in(xp, g, w, tm_p)
            r3 = lambda a: a.reshape(b, s, a.shape[-1])
            o = _sb_prompt(r3(q), r3(k), r3(v), t)
            xp = _out_proj(xp, [(o.reshape(b * s, SB_W), z)], w_out, gfin, final, tm_o)
            outs["sb_p"].append(kv.reshape(b, s, 2, SB_H, HEAD_DIM))
            q, kv, k, v, z = _odd_in(xs, g, w, bd)
            o = _sb_decode(page_table, _head_rows(q, SB_H), cache_sb_t, li, n_pages).reshape(bd, SB_W)
            xs = _out_proj(xs, [(o, z)], w_out, gfin, final, bd)
            outs["sb_s"].append(kv.reshape(bd, 1, 2, SB_H, HEAD_DIM))

    st = lambda k: jnp.stack(outs[k], 1)
    return (xp.reshape(b, s, d), xs.reshape(bd, 1, d), st("nsa_p"), st("nsa_s"), st("win_p"), st("win_s"),
            st("fox_p"), st("fox_s"), st("lf_p"), st("lf_s"), st("sb_p"), st("sb_s"))
```

```python
import functools
import math

import numpy as np
import jax
import jax.numpy as jnp
from jax import lax
from jax.experimental import pallas as pl
from jax.experimental.pallas import tpu as pltpu

F32 = jnp.float32
BF16 = jnp.bfloat16

D_MODEL = 1024
HEAD_DIM = 64
LANES = 128
NSA_H = 8
NSA_KVH = 2
NSA_GROUP = NSA_H // NSA_KVH
FOX_H = 8
SB_H = 16
NSA_W = NSA_H * HEAD_DIM
FOX_W = FOX_H * HEAD_DIM
SB_W = SB_H * HEAD_DIM
CMP_BLOCK = 64
TOP_N = 16
WINDOW = 512
PAGE_SIZE = 128
N_BUCKETS = 32
REL_MAX_DIST = 128
RMS_EPS = 1e-6
NEG_INF = -1e30
SCALE = HEAD_DIM ** -0.5
EXP_ZERO = 104.0

ATT_T = 256
SB_T = 256
DECODE_PAGES = 8
SB_DECODE_PAGES = 2
VMEM_LIMIT = 56 * 1024 * 1024


def _cparams(*sem):
    return pltpu.CompilerParams(dimension_semantics=sem, vmem_limit_bytes=VMEM_LIMIT)


def _dot(a, b):
    return jnp.dot(a, b, preferred_element_type=F32)


def _dot_nt(a, b):
    return lax.dot_general(a, b, (((1,), (1,)), ((), ())), preferred_element_type=F32)


def _split2(x):
    hi = x.astype(BF16)
    lo = (x - hi.astype(F32)).astype(BF16)
    return hi, lo


def _dot_exact01(x, w01, passes, left=False):
    acc = None
    r = x
    for _ in range(passes):
        piece = r.astype(BF16)
        term = _dot(w01, piece) if left else _dot(piece, w01)
        acc = term if acc is None else acc + term
        r = r - piece.astype(F32)
    return acc


def _softplus(z):
    return jnp.maximum(z, 0.0) + jnp.log1p(jnp.exp(-jnp.abs(z)))


def _sigmoid(z):
    return 1.0 / (1.0 + jnp.exp(-z))


def _rms(x, g):
    return x * lax.rsqrt(jnp.mean(x * x, axis=-1, keepdims=True) + RMS_EPS) * g


def _lane_lo(shape):
    return lax.broadcasted_iota(jnp.int32, shape, len(shape) - 1) < HEAD_DIM


def _dup_halves(x):
    r = pltpu.roll(x, HEAD_DIM, 1)
    lo = _lane_lo(x.shape)
    return jnp.where(lo, x, r), jnp.where(lo, r, x)


E_Q, E_ROWS, E_WIN, E_NZ, E_FQ, E_FK, E_FV, E_FZ, E_SMALL, E_END = (
    0, 512, 1024, 1280, 1792, 2304, 2816, 3328, 3840, 3968)
SMALL_GATE = 3 * NSA_H


def _even_in_kernel(x_ref, g_ref, w_ref, bf_ref, q_ref, rows_ref, selkv_ref, win_ref, winkv_ref, small_ref,
                    nz_ref, fq_ref, fkv_ref, fk_ref, fv_ref, fz_ref):
    xn = _rms(x_ref[...], g_ref[...]).astype(BF16)

    def mm(a, b):
        return _dot(xn, w_ref[:, a:b])

    q_ref[...] = (mm(E_Q, E_ROWS) * SCALE).astype(BF16)
    rows = mm(E_ROWS, E_WIN)
    rows_ref[...] = rows
    kd0, kd1 = _dup_halves(rows[:, 256:384])
    vd0, vd1 = _dup_halves(rows[:, 384:512])
    selkv_ref[...] = jnp.concatenate([kd0, kd1, vd0, vd1], axis=1).astype(BF16)
    win = mm(E_WIN, E_NZ)
    win_ref[...] = win
    kd0, kd1 = _dup_halves(win[:, 0:128])
    vd0, vd1 = _dup_halves(win[:, 128:256])
    winkv_ref[...] = jnp.concatenate([kd0, kd1, vd0, vd1], axis=1).astype(BF16)
    nz_ref[...] = mm(E_NZ, E_FQ)
    fq_ref[...] = (mm(E_FQ, E_FK) * SCALE).astype(BF16)
    fkv = mm(E_FK, E_FZ)
    fkv_ref[...] = fkv
    fk_ref[...] = fkv[:, :FOX_W].astype(BF16)
    fv_ref[...] = fkv[:, FOX_W:].astype(BF16)
    fz_ref[...] = mm(E_FZ, E_SMALL)
    small = mm(E_SMALL, E_END)
    lane = lax.broadcasted_iota(jnp.int32, small.shape, 1)
    is_f = (lane >= SMALL_GATE) & (lane < SMALL_GATE + FOX_H)
    small_ref[...] = jnp.where(is_f, -_softplus(-(small + bf_ref[...])), small)


def _even_in(x, g, w, bfp, tm):
    m = x.shape[0]
    row = lambda c: pl.BlockSpec((tm, c), lambda i: (i, 0))
    full = lambda a: pl.BlockSpec(a.shape, lambda i: (0, 0))
    outs = [(512, BF16), (512, F32), (512, BF16), (256, F32), (512, BF16), (128, F32),
            (512, F32), (512, BF16), (1024, F32), (512, BF16), (512, BF16), (512, F32)]
    return pl.pallas_call(
        _even_in_kernel,
        grid=(m // tm,),
        in_specs=[row(D_MODEL), full(g), full(w), full(bfp)],
        out_specs=[row(c) for c, _ in outs],
        out_shape=[jax.ShapeDtypeStruct((m, c), dt) for c, dt in outs],
        compiler_params=_cparams("parallel"),
        name="even_in",
    )(x, g, w, bfp)


def _odd_in_kernel(x_ref, g_ref, w_ref, q_ref, kv_ref, k_ref, v_ref, z_ref):
    xn = _rms(x_ref[...], g_ref[...]).astype(BF16)
    q_ref[...] = (_dot(xn, w_ref[:, 0:SB_W]) * SCALE).astype(BF16)
    kv = _dot(xn, w_ref[:, SB_W:3 * SB_W])
    kv_ref[...] = kv
    k_ref[...] = kv[:, :SB_W].astype(BF16)
    v_ref[...] = kv[:, SB_W:].astype(BF16)
    z_ref[...] = _dot(xn, w_ref[:, 3 * SB_W:])


def _odd_in(x, g, w, tm):
    m = x.shape[0]
    row = lambda c: pl.BlockSpec((tm, c), lambda i: (i, 0))
    full = lambda a: pl.BlockSpec(a.shape, lambda i: (0, 0))
    outs = [(SB_W, BF16), (2 * SB_W, F32), (SB_W, BF16), (SB_W, BF16), (SB_W, F32)]
    return pl.pallas_call(
        _odd_in_kernel,
        grid=(m // tm,),
        in_specs=[row(D_MODEL), full(g), full(w)],
        out_specs=[row(c) for c, _ in outs],
        out_shape=[jax.ShapeDtypeStruct((m, c), dt) for c, dt in outs],
        compiler_params=_cparams("parallel"),
        name="odd_in",
    )(x, g, w)


def _out_kernel(n_parts, final, *refs):
    x_ref = refs[0]
    parts = refs[1:1 + 2 * n_parts]
    w_ref = refs[1 + 2 * n_parts]
    gf_ref = refs[2 + 2 * n_parts]
    y_ref = refs[3 + 2 * n_parts]
    y = x_ref[...]
    off = 0
    for p in range(n_parts):
        o = parts[2 * p][...]
        z = parts[2 * p + 1][...]
        k = o.shape[1]
        y = y + _dot((o * (z * _sigmoid(z))).astype(BF16), w_ref[off:off + k, :])
        off += k
    if final:
        y = _rms(y, gf_ref[...])
    y_ref[...] = y


def _out_proj(x, parts, w, gf, final, tm):
    m = x.shape[0]
    row = lambda c: pl.BlockSpec((tm, c), lambda i: (i, 0))
    full = lambda a: pl.BlockSpec(a.shape, lambda i: (0, 0))
    flat = [a for pr in parts for a in pr]
    return pl.pallas_call(
        functools.partial(_out_kernel, len(parts), final),
        grid=(m // tm,),
        in_specs=[row(D_MODEL)] + [row(a.shape[1]) for a in flat] + [full(w), full(gf)],
        out_specs=row(D_MODEL),
        out_shape=jax.ShapeDtypeStruct((m, D_MODEL), F32),
        compiler_params=_cparams("parallel"),
        name="out_proj",
    )(x, *flat, w, gf)


def _cumsum_kernel(x_ref, o_ref, *, chunk):
    n = x_ref.shape[2] // chunk
    r = lax.broadcasted_iota(jnp.int32, (chunk, chunk), 0)
    c = lax.broadcasted_iota(jnp.int32, (chunk, chunk), 1)
    tri = jnp.where(r <= c, 1.0, 0.0).astype(BF16)

    def body(j, carry):
        x = x_ref[0, :, pl.ds(j * chunk, chunk)]
        cs = _dot_exact01(x, tri, 3) + carry
        o_ref[0, :, pl.ds(j * chunk, chunk)] = cs
        return cs[:, chunk - 1:chunk]

    lax.fori_loop(0, n, body, jnp.zeros((x_ref.shape[1], 1), F32))


def _cumsum_lanes(x):
    b, h, s = x.shape
    spec = pl.BlockSpec((1, h, s), lambda i: (i, 0, 0))
    return pl.pallas_call(
        functools.partial(_cumsum_kernel, chunk=min(256, s)),
        grid=(b,),
        in_specs=[spec],
        out_specs=spec,
        out_shape=jax.ShapeDtypeStruct(x.shape, F32),
        compiler_params=_cparams("parallel"),
        name="logf_cumsum",
    )(x)


def _fox_prompt_kernel(q_ref, k_ref, v_ref, c_ref, o_ref, kmax_sc, *, t):
    i = pl.program_id(2)
    q2 = q_ref[0]
    lo = _lane_lo(q2.shape)
    zero = jnp.zeros_like(q2)
    qs = (jnp.where(lo, q2, zero), jnp.where(lo, zero, q2))
    row = lax.broadcasted_iota(jnp.int32, (t, t), 0)
    col = lax.broadcasted_iota(jnp.int32, (t, t), 1)
    causal = row >= col

    @pl.when(i == 0)
    def _():
        def norm_step(cidx, mx):
            kk = k_ref[0, pl.ds(cidx * t, t), :].astype(F32)
            kk = kk * kk
            n0 = jnp.max(jnp.sum(jnp.where(lo, kk, 0.0), axis=1, keepdims=True), axis=0, keepdims=True)
            n1 = jnp.max(jnp.sum(jnp.where(lo, 0.0, kk), axis=1, keepdims=True), axis=0, keepdims=True)
            return jnp.maximum(mx[0], n0), jnp.maximum(mx[1], n1)

        z11 = jnp.zeros((1, 1), F32)
        mx = lax.fori_loop(0, k_ref.shape[1] // t, norm_step, (z11, z11))
        for h in range(2):
            kmax_sc[h] = jnp.broadcast_to(jnp.sqrt(mx[h]), kmax_sc.shape[1:])

    qf = q2.astype(F32)
    qq = qf * qf
    bound = []
    for h in range(2):
        qn = jnp.sqrt(jnp.sum(jnp.where(lo, qq, 0.0) if h == 0 else jnp.where(lo, 0.0, qq), axis=1, keepdims=True))
        bound.append(qn * kmax_sc[h][0:1, 0:1] + 1.0)

    def step(j, carry, masked):
        kblk = k_ref[0, pl.ds(j * t, t), :]
        vblk = v_ref[0, pl.ds(j * t, t), :]
        cb = c_ref[0, 0, :, pl.ds(j * t, t)]
        new = []
        for h in range(2):
            m_old, l_old, acc = carry[h]
            s = _dot_nt(qs[h], kblk) - cb[h:h + 1, :]
            if masked:
                s = jnp.where(causal, s, NEG_INF)
            m_new = jnp.maximum(m_old, jnp.max(s, axis=1, keepdims=True))
            alpha = jnp.exp(m_old - m_new)
            p = jnp.exp(s - m_new)
            if masked:
                p = jnp.where(causal, p, 0.0)
            l_new = alpha * l_old + jnp.sum(p, axis=1, keepdims=True)
            acc = alpha * acc + _dot(p.astype(BF16), vblk)
            new.append((m_new, l_new, acc))
        return tuple(new)

    def older_tiles_matter(jj, carry):
        cend = c_ref[0, 0, :, pl.ds(jnp.maximum(jj, 0) * t, t)][:, t - 1:t]
        gap = [jnp.max(bound[h] - carry[h][0], axis=0, keepdims=True) - cend[h:h + 1, :] for h in range(2)]
        return (jnp.max(jnp.maximum(gap[0], gap[1])) >= -EXP_ZERO).astype(jnp.int32)

    def body(state):
        jj, _, carry = state
        carry = step(jj, carry, False)
        return jj - 1, older_tiles_matter(jj - 1, carry), carry

    init = tuple((jnp.full((t, 1), NEG_INF, F32), jnp.zeros((t, 1), F32), jnp.zeros((t, LANES), F32))
                 for _ in range(2))
    carry = step(i, init, True)
    _, _, carry = lax.while_loop(lambda st: (st[0] >= 0) & (st[1] > 0), body,
                                 (i - 1, older_tiles_matter(i - 1, carry), carry))
    outs = [acc / jnp.maximum(l, 1e-30) for (_, l, acc) in carry]
    o_ref[0] = jnp.where(lo, outs[0], outs[1])


def _fox_prompt(q, k, v, c4, t):
    b, s, w = q.shape
    hp = w // LANES
    return pl.pallas_call(
        functools.partial(_fox_prompt_kernel, t=t),
        grid=(b, hp, s // t),
        in_specs=[pl.BlockSpec((1, t, LANES), lambda bi, h, i: (bi, i, h)),
                  pl.BlockSpec((1, s, LANES), lambda bi, h, i: (bi, 0, h)),
                  pl.BlockSpec((1, s, LANES), lambda bi, h, i: (bi, 0, h)),
                  pl.BlockSpec((1, 1, 2, s), lambda bi, h, i: (bi, h, 0, 0))],
        out_specs=pl.BlockSpec((1, t, LANES), lambda bi, h, i: (bi, i, h)),
        out_shape=jax.ShapeDtypeStruct((b, s, w), F32),
        scratch_shapes=[pltpu.VMEM((2, 8, LANES), F32)],
        compiler_params=_cparams("parallel", "parallel", "arbitrary"),
        name="fox_prompt",
    )(q, k, v, c4)


def _sb_weights(z, after_carry, ustrict, mask):
    sp = _softplus(z)
    l1m = -sp if mask is None else jnp.where(mask, -sp, 0.0)
    after = _dot_exact01(l1m, ustrict, 2) + after_carry
    a = jnp.exp(z - sp + after)
    if mask is not None:
        a = jnp.where(mask, a, 0.0)
    return a, after_carry + jnp.sum(l1m, axis=1, keepdims=True)


def _sb_prompt_kernel(q_ref, k_ref, v_ref, o_ref, *, t):
    i = pl.program_id(2)
    q2 = q_ref[0]
    lo = _lane_lo(q2.shape)
    zero = jnp.zeros_like(q2)
    qs = (jnp.where(lo, q2, zero), jnp.where(lo, zero, q2))
    row = lax.broadcasted_iota(jnp.int32, (t, t), 0)
    col = lax.broadcasted_iota(jnp.int32, (t, t), 1)
    strict = row > col
    ustrict = jnp.where(strict, 1.0, 0.0).astype(BF16)

    def step(j, carry, mask):
        kblk = k_ref[0, pl.ds(j * t, t), :]
        vblk = v_ref[0, pl.ds(j * t, t), :]
        new = []
        for h in range(2):
            run, acc = carry[h]
            a, run = _sb_weights(_dot_nt(qs[h], kblk), run, ustrict, mask)
            new.append((run, acc + _dot(a.astype(BF16), vblk)))
        return tuple(new)

    def older_tiles_matter(carry):
        return (jnp.max(jnp.maximum(carry[0][0], carry[1][0])) >= -EXP_ZERO).astype(jnp.int32)

    def body(state):
        jj, _, carry = state
        carry = step(jj, carry, None)
        return jj - 1, older_tiles_matter(carry), carry

    init = tuple((jnp.zeros((t, 1), F32), jnp.zeros((t, LANES), F32)) for _ in range(2))
    carry = step(i, init, strict)
    _, _, carry = lax.while_loop(lambda st: (st[0] >= 0) & (st[1] > 0), body,
                                 (i - 1, older_tiles_matter(carry), carry))
    o_ref[0] = jnp.where(lo, carry[0][1], carry[1][1])


def _sb_prompt(q, k, v, t):
    b, s, w = q.shape
    hp = w // LANES
    return pl.pallas_call(
        functools.partial(_sb_prompt_kernel, t=t),
        grid=(b, hp, s // t),
        in_specs=[pl.BlockSpec((1, t, LANES), lambda bi, h, i: (bi, i, h)),
                  pl.BlockSpec((1, s, LANES), lambda bi, h, i: (bi, 0, h)),
                  pl.BlockSpec((1, s, LANES), lambda bi, h, i: (bi, 0, h))],
        out_specs=pl.BlockSpec((1, t, LANES), lambda bi, h, i: (bi, i, h)),
        out_shape=jax.ShapeDtypeStruct((b, s, w), F32),
        compiler_params=_cparams("parallel", "parallel", "arbitrary"),
        name="sb_prompt",
    )(q, k, v)


def _bucket_starts():
    d = np.arange(0, 4 * REL_MAX_DIST)
    max_exact = N_BUCKETS // 2
    large = max_exact + (np.log(np.maximum(d, 1).astype(np.float32) / max_exact)
                         / math.log(REL_MAX_DIST / max_exact) * (N_BUCKETS - max_exact)).astype(np.int32)
    b = np.where(d < max_exact, d, np.minimum(large, N_BUCKETS - 1))
    return [int(np.min(np.nonzero(b >= k)[0])) for k in range(N_BUCKETS)]


BUCKET_STARTS = _bucket_starts()
BIAS_FAR_DIST = BUCKET_STARTS[-1]


def _bias_from_dist(dist, tab):
    v = jnp.where(dist >= BUCKET_STARTS[1], tab(1), tab(0))
    for b in range(2, N_BUCKETS):
        v = jnp.where(dist >= BUCKET_STARTS[b], tab(b), v)
    return v


def _masked_softmax(s, mask):
    logits = jnp.where(mask, s, NEG_INF)
    m = jnp.max(logits, axis=-1, keepdims=True)
    e = jnp.where(mask, jnp.exp(logits - m), 0.0)
    return e / jnp.maximum(jnp.sum(e, axis=-1, keepdims=True), 1e-30)


def _top_blocks(score, n_top):
    lanef = lax.broadcasted_iota(jnp.int32, score.shape, 1).astype(F32)
    work = score
    msel = jnp.zeros(score.shape, F32)
    for _ in range(n_top):
        mx = jnp.max(work, axis=1, keepdims=True)
        idx = jnp.min(jnp.where(work == mx, lanef, 1e9), axis=1, keepdims=True)
        pick = lanef == idx
        msel = jnp.where(pick, 1.0, msel)
        work = jnp.where(pick, -2.0, work)
    return msel


def _compress_rows(k_ref, v_ref, nb, pe_ref, w1_ref, w2_ref):
    outs = []
    for kv, ref in enumerate((k_ref, v_ref)):
        acc = jnp.zeros((nb, LANES), F32)
        for pos in range(CMP_BLOCK):
            x = ref[pl.ds(pos, nb, stride=CMP_BLOCK), :] + pe_ref[kv, pos:pos + 1, :]
            acc = acc + _dot(x.astype(BF16), w1_ref[kv, pos])
        hid = acc * _sigmoid(acc)
        outs.append(_dot(hid.astype(BF16), w2_ref[kv]))
    return outs


def _compress_kernel(k_ref, v_ref, pe_ref, w1_ref, w2_ref, o_ref):
    nb = o_ref.shape[1]
    kc, vc = _compress_rows(k_ref.at[0], v_ref.at[0], nb, pe_ref, w1_ref, w2_ref)
    o_ref[0] = jnp.concatenate([kc, vc], axis=1).astype(BF16)


def _compress_prompt(rows, pe, w1bd, w2big):
    b, s, _ = rows.shape
    nb = s // CMP_BLOCK
    full = lambda a: pl.BlockSpec(a.shape, lambda i: (0,) * a.ndim)
    return pl.pallas_call(
        _compress_kernel,
        grid=(b,),
        in_specs=[pl.BlockSpec((1, s, LANES), lambda i: (i, 0, 0)), pl.BlockSpec((1, s, LANES), lambda i: (i, 0, 1)),
                  full(pe), full(w1bd), full(w2big)],
        out_specs=pl.BlockSpec((1, nb, 4 * LANES), lambda i: (i, 0, 0)),
        out_shape=jax.ShapeDtypeStruct((b, nb, 4 * LANES), BF16),
        compiler_params=_cparams("parallel"),
        name="nsa_compress_prompt",
    )(rows, rows, pe, w1bd, w2big)


def _nsa_prompt_kernel(tab_ref, q_ref, ks_ref, vs_ref, kw_ref, vw_ref, kc_ref, vc_ref, gate_ref, o_ref, tiles_sc,
                       cmpb_sc, *, t, n_top):
    g = pl.program_id(1)
    i = pl.program_id(2)
    nh = NSA_GROUP
    nb = kc_ref.shape[1]
    bpt = t // CMP_BLOCK
    row = lax.broadcasted_iota(jnp.int32, (t, t), 0)
    col = lax.broadcasted_iota(jnp.int32, (t, t), 1)
    rowc = lax.broadcasted_iota(jnp.int32, (t, nb), 0)
    colc = lax.broadcasted_iota(jnp.int32, (t, nb), 1)

    @pl.when(i == 0)
    def _():
        for hh in range(nh):
            h = g * nh + hh
            tab = lambda b: tab_ref[b, h]
            far = tab_ref[N_BUCKETS - 1, h]
            for kind in range(2):
                tiles_sc[kind, hh] = _bias_from_dist(kind * t + row - col, tab) - far
            rel = jnp.where(colc < nb // 2, colc, colc - nb)
            dist0 = rowc - (rel * CMP_BLOCK + (CMP_BLOCK - 1))
            cmpb_sc[hh] = jnp.where(dist0 >= 0, _bias_from_dist(dist0, tab) - far, 0.0)

    q4 = q_ref[0]
    lo = _lane_lo((t, LANES))
    zero = jnp.zeros((t, LANES), BF16)
    qa, qb = q4[:, :LANES], q4[:, LANES:]
    qst = jnp.concatenate([jnp.where(lo, qa, zero), jnp.where(lo, zero, qa),
                           jnp.where(lo, qb, zero), jnp.where(lo, zero, qb)], axis=0)

    dist_c = (i * t + rowc) - (colc * CMP_BLOCK + (CMP_BLOCK - 1))
    bias_c = jnp.stack([pltpu.roll(cmpb_sc[hh], bpt * i, 1) for hh in range(nh)], axis=0)
    valid_c = (dist_c >= 0)[None]
    p_c = _masked_softmax(_dot_nt(qst, kc_ref[0]).reshape(nh, t, nb) + bias_c, valid_c)
    o_c = _dot(p_c.reshape(nh * t, nb).astype(BF16), vc_ref[0])

    cur = (i * t + rowc) // CMP_BLOCK
    forced = (colc == 0) | (colc == cur) | (colc == cur - 1)
    score = jnp.where(forced, NSA_GROUP + 1.0, jnp.where(colc <= cur, jnp.sum(p_c, axis=0), -1.0))
    msel = _top_blocks(score, n_top).astype(BF16)

    blk_of_key = lax.broadcasted_iota(jnp.int32, (nb, t), 0)
    key_in_tile = lax.broadcasted_iota(jnp.int32, (nb, t), 1)

    def flash_step(carry, j, k_ref, v_ref, mask, bias):
        m_old, l_old, acc = carry
        kblk = k_ref[0, pl.ds(j * t, t), :]
        vblk = v_ref[0, pl.ds(j * t, t), :]
        s = _dot_nt(qst, kblk).reshape(nh, t, t)
        if bias is not None:
            s = s + bias
        s = jnp.where(mask[None], s, NEG_INF)
        m_new = jnp.maximum(m_old, jnp.max(s, axis=-1, keepdims=True))
        alpha = jnp.exp(m_old - m_new)
        p = jnp.exp(s - m_new)
        l_new = alpha * l_old + jnp.sum(p, axis=-1, keepdims=True)
        acc = alpha.reshape(nh * t, 1) * acc + _dot(p.reshape(nh * t, t).astype(BF16), vblk)
        return m_new, l_new, acc

    def finish(carry):
        _, l, acc = carry
        return acc / jnp.maximum(l, 1e-30).reshape(nh * t, 1)

    def selected(j, on):
        expand = jnp.where(blk_of_key == jnp.right_shift(j * t + key_in_tile, 6), 1.0, 0.0).astype(BF16)
        return _dot(msel, expand) > jnp.where(on, 0.5, 2.0)

    dist0 = row - col
    prev = jnp.maximum(i - 1, 0)
    init = (jnp.full((nh, t, 1), NEG_INF, F32), jnp.zeros((nh, t, 1), F32), jnp.zeros((nh * t, LANES), F32))
    carry = flash_step(init, i, ks_ref, vs_ref, selected(i, True) & (dist0 >= 0), tiles_sc[0])
    carry = flash_step(carry, prev, ks_ref, vs_ref, selected(prev, i >= 1), tiles_sc[1])
    o_s = finish(lax.fori_loop(0, prev, lambda j, c: flash_step(c, j, ks_ref, vs_ref, selected(j, True), None), carry))

    carry = flash_step(init, i, kw_ref, vw_ref, (dist0 >= 0) & (dist0 <= WINDOW), tiles_sc[0])
    for k in range(1, (WINDOW + t - 1) // t + 1):
        reach = jnp.where(i >= k, WINDOW - k * t, -t - 1)
        carry = flash_step(carry, jnp.maximum(i - k, 0), kw_ref, vw_ref, dist0 <= reach,
                           tiles_sc[1] if k == 1 else None)
    o_w = finish(carry)

    sig = _sigmoid(gate_ref[0])
    lane = lax.broadcasted_iota(jnp.int32, sig.shape, 1)

    def gate(hh, br):
        return jnp.sum(jnp.where(lane == (g * nh + hh) * 3 + br, sig, 0.0), axis=1, keepdims=True)

    outs = []
    for hh in range(nh):
        sl = slice(hh * t, (hh + 1) * t)
        outs.append(gate(hh, 0) * o_c[sl] + gate(hh, 1) * o_s[sl] + gate(hh, 2) * o_w[sl])
    o_ref[0] = jnp.concatenate([jnp.where(lo, outs[0], outs[1]), jnp.where(lo, outs[2], outs[3])], axis=1)


def _nsa_prompt(tab, q, selkv, winkv, cmp, small, t):
    b, s, _ = q.shape
    nb = cmp.shape[1]
    assert t >= BIAS_FAR_DIST and s % t == 0
    kv = lambda off: pl.BlockSpec((1, s, LANES), lambda bi, g, i: (bi, 0, off + g))
    cm = lambda off: pl.BlockSpec((1, nb, LANES), lambda bi, g, i: (bi, 0, off + g))
    return pl.pallas_call(
        functools.partial(_nsa_prompt_kernel, t=t, n_top=min(TOP_N, nb)),
        grid=(b, NSA_KVH, s // t),
        in_specs=[pl.BlockSpec(memory_space=pltpu.SMEM),
                  pl.BlockSpec((1, t, 2 * LANES), lambda bi, g, i: (bi, i, g)),
                  kv(0), kv(2), kv(0), kv(2), cm(0), cm(2),
                  pl.BlockSpec((1, t, LANES), lambda bi, g, i: (bi, i, 0))],
        out_specs=pl.BlockSpec((1, t, 2 * LANES), lambda bi, g, i: (bi, i, g)),
        out_shape=jax.ShapeDtypeStruct((b, s, NSA_W), F32),
        scratch_shapes=[pltpu.VMEM((2, NSA_GROUP, t, t), F32), pltpu.VMEM((NSA_GROUP, t, nb), F32)],
        compiler_params=_cparams("parallel", "parallel", "arbitrary"),
        name="nsa_prompt",
    )(tab, q, selkv, selkv, winkv, winkv, cmp, cmp, small)


def _head_diag(x, width):
    r = lax.broadcasted_iota(jnp.int32, x.shape, 0)
    c = lax.broadcasted_iota(jnp.int32, x.shape, 1)
    own = (c >= r * width) & (c < (r + 1) * width)
    return jnp.sum(jnp.where(own, x, 0.0), axis=0, keepdims=True)


def _page_specs(n, block, page_of, li):
    return [pl.BlockSpec(block, functools.partial(
        lambda b, p, pt, g: (pt[b, page_of(p, g)], li) + (0,) * (len(block) - 2), g=g)) for g in range(n)]


def _fox_decode_kernel(pt_ref, q_ref, *refs, n_pages):
    page_refs = refs[:n_pages]
    lf_refs = refs[n_pages:2 * n_pages]
    kvn_ref, lfn_ref, o_ref, m_sc, l_sc, acc_sc, c_sc = refs[2 * n_pages:]
    p = pl.program_id(1)
    w = FOX_W

    @pl.when(p == 0)
    def _():
        m_sc[...] = jnp.full(m_sc.shape, NEG_INF, F32)
        l_sc[...] = jnp.zeros(l_sc.shape, F32)
        acc_sc[...] = jnp.zeros(acc_sc.shape, F32)
        c_sc[...] = jnp.zeros(c_sc.shape, F32)

    q = q_ref[0]
    r = lax.broadcasted_iota(jnp.int32, (PAGE_SIZE, PAGE_SIZE), 0)
    cidx = lax.broadcasted_iota(jnp.int32, (PAGE_SIZE, PAGE_SIZE), 1)
    tri = jnp.where(r <= cidx, 1.0, 0.0).astype(BF16)
    m, l, acc, c_run = m_sc[...], l_sc[...], acc_sc[...], c_sc[...]
    scores = []
    for page_ref, lf_ref in zip(page_refs, lf_refs):
        lf = lf_ref[0, 0]
        c = _dot_exact01(lf, tri, 3) + c_run
        c_run = c_run + jnp.sum(lf, axis=1, keepdims=True)
        scores.append(_dot(q, page_ref[0, 0, 0].astype(BF16)) - c)
    s = jnp.concatenate(scores, axis=1)
    m_new = jnp.maximum(m, jnp.max(s, axis=1, keepdims=True))
    alpha = jnp.exp(m - m_new)
    e = jnp.exp(s - m_new)
    l = alpha * l + jnp.sum(e, axis=1, keepdims=True)
    acc = alpha * acc
    for g, page_ref in enumerate(page_refs):
        acc = acc + _dot_nt(e[:, g * PAGE_SIZE:(g + 1) * PAGE_SIZE].astype(BF16), page_ref[0, 0, 1].astype(BF16))
    m = m_new
    m_sc[...], l_sc[...], acc_sc[...], c_sc[...] = m, l, acc, c_run

    @pl.when(p == pl.num_programs(1) - 1)
    def _():
        kvn = kvn_ref[0].astype(BF16).astype(F32)
        s_n = jnp.sum(q.astype(F32) * kvn[:, :w], axis=1, keepdims=True) - (c_run + lfn_ref[0])
        m_fin = jnp.maximum(m, s_n)
        a = jnp.exp(m - m_fin)
        e_n = jnp.exp(s_n - m_fin)
        l_fin = a * l + e_n
        out = a * acc + e_n.astype(BF16).astype(F32) * kvn[:, w:]
        o_ref[0] = _head_diag(out / jnp.maximum(l_fin, 1e-30), HEAD_DIM)


def _fox_decode(page_table, qrows, cache_t, lft, kv_new, lf_new, li, n_pages):
    bd, npg = page_table.shape
    h = FOX_H
    assert npg % n_pages == 0
    page_of = lambda p, g: p * n_pages + g
    grid_spec = pltpu.PrefetchScalarGridSpec(
        num_scalar_prefetch=1,
        grid=(bd, npg // n_pages),
        in_specs=[pl.BlockSpec((1, h, FOX_W), lambda b, p, pt: (b, 0, 0))]
        + _page_specs(n_pages, (1, 1, 2, FOX_W, PAGE_SIZE), page_of, li)
        + _page_specs(n_pages, (1, 1, h, PAGE_SIZE), page_of, li)
        + [pl.BlockSpec((1, 1, 2 * FOX_W), lambda b, p, pt: (b, 0, 0)),
           pl.BlockSpec((1, h, 1), lambda b, p, pt: (b, 0, 0))],
        out_specs=pl.BlockSpec((1, 1, FOX_W), lambda b, p, pt: (b, 0, 0)),
        scratch_shapes=[pltpu.VMEM((h, 1), F32), pltpu.VMEM((h, 1), F32), pltpu.VMEM((h, FOX_W), F32),
                        pltpu.VMEM((h, 1), F32)])
    return pl.pallas_call(
        functools.partial(_fox_decode_kernel, n_pages=n_pages),
        grid_spec=grid_spec,
        out_shape=jax.ShapeDtypeStruct((bd, 1, FOX_W), F32),
        compiler_params=_cparams("parallel", "arbitrary"),
        name="fox_decode",
    )(page_table, qrows, *([cache_t] * n_pages), *([lft] * n_pages), kv_new, lf_new)


def _sb_decode_kernel(pt_ref, q_ref, cache_ref, o_ref, buf, sem, *, li, n_pages):
    b = pl.program_id(0)
    npg = pt_ref.shape[1]
    n_chunks = npg // n_pages

    def page_copy(c, g, slot):
        page = pt_ref[b, npg - 1 - (c * n_pages + g)]
        return pltpu.make_async_copy(cache_ref.at[page, li], buf.at[slot, g], sem.at[slot])

    def start(c, slot):
        for g in range(n_pages):
            page_copy(c, g, slot).start()

    def wait(c, slot):
        for g in range(n_pages):
            page_copy(c, g, slot).wait()

    r = lax.broadcasted_iota(jnp.int32, (PAGE_SIZE, PAGE_SIZE), 0)
    cidx = lax.broadcasted_iota(jnp.int32, (PAGE_SIZE, PAGE_SIZE), 1)
    ustrict = jnp.where(r > cidx, 1.0, 0.0).astype(BF16)
    q = q_ref[0]

    def body(state):
        c, _, run, acc = state
        slot = lax.rem(c, 2)

        @pl.when(c + 1 < n_chunks)
        def _():
            start(c + 1, 1 - slot)

        wait(c, slot)
        for g in range(n_pages):
            a, run = _sb_weights(_dot(q, buf[slot, g, 0].astype(BF16)), run, ustrict, None)
            acc = acc + _dot_nt(a.astype(BF16), buf[slot, g, 1].astype(BF16))
        return c + 1, (jnp.max(run) >= -EXP_ZERO).astype(jnp.int32), run, acc

    start(0, 0)
    c_end, _, _, acc = lax.while_loop(
        lambda st: (st[0] < n_chunks) & (st[1] > 0), body,
        (jnp.int32(0), jnp.int32(1), jnp.zeros((SB_H, 1), F32), jnp.zeros((SB_H, SB_W), F32)))

    @pl.when(c_end < n_chunks)
    def _():
        wait(c_end, lax.rem(c_end, 2))

    o_ref[0] = _head_diag(acc, HEAD_DIM)


def _sb_decode(page_table, qrows, cache_t, li, n_pages):
    bd, npg = page_table.shape
    assert npg % n_pages == 0
    grid_spec = pltpu.PrefetchScalarGridSpec(
        num_scalar_prefetch=1,
        grid=(bd,),
        in_specs=[pl.BlockSpec((1, SB_H, SB_W), lambda b, pt: (b, 0, 0)), pl.BlockSpec(memory_space=pl.ANY)],
        out_specs=pl.BlockSpec((1, 1, SB_W), lambda b, pt: (b, 0, 0)),
        scratch_shapes=[pltpu.VMEM((2, n_pages, 2, SB_W, PAGE_SIZE), F32), pltpu.SemaphoreType.DMA((2,))])
    return pl.pallas_call(
        functools.partial(_sb_decode_kernel, li=li, n_pages=n_pages),
        grid_spec=grid_spec,
        out_shape=jax.ShapeDtypeStruct((bd, 1, SB_W), F32),
        compiler_params=_cparams("arbitrary"),
        name="sb_decode",
    )(page_table, qrows, cache_t)


def _nsa_decode_kernel(pt_ref, q_ref, *refs, n_pages, n_top):
    page_refs = refs[:n_pages]
    (win_ref, new_ref, gate_ref, tab_ref, expand_ref, pe_ref, w1_ref, w2_ref, o_ref,
     kc_sc, vc_sc, ks_sc, vs_sc) = refs[n_pages:]
    p = pl.program_id(1)
    past = kc_sc.shape[0]
    nb = past // CMP_BLOCK
    for g, page_ref in enumerate(page_refs):
        rows = pl.ds(pl.multiple_of((p * n_pages + g) * PAGE_SIZE, PAGE_SIZE), PAGE_SIZE)
        kc_sc[rows, :] = page_ref[0, 0, 0].T
        vc_sc[rows, :] = page_ref[0, 0, 1].T
        ks_sc[:, rows] = page_ref[0, 0, 2].astype(BF16)
        vs_sc[:, rows] = page_ref[0, 0, 3].astype(BF16)

    @pl.when(p == pl.num_programs(1) - 1)
    def _():
        q = q_ref[0]
        qf = q.astype(F32)
        tab = lambda b: tab_ref[:, b:b + 1]
        new = new_ref[0].astype(BF16).astype(F32)

        def with_new(s, mask, k_new, v_new, v_past_t):
            s_n = jnp.sum(qf * k_new, axis=1, keepdims=True) + tab(0)
            s = jnp.where(mask, s, NEG_INF)
            m = jnp.maximum(jnp.max(s, axis=1, keepdims=True), s_n)
            e = jnp.where(mask, jnp.exp(s - m), 0.0)
            e_n = jnp.exp(s_n - m)
            l = jnp.sum(e, axis=1, keepdims=True) + e_n
            acc = _dot_nt(e.astype(BF16), v_past_t) + e_n.astype(BF16).astype(F32) * v_new
            return acc / jnp.maximum(l, 1e-30)

        kc, vc = _compress_rows(kc_sc, vc_sc, nb, pe_ref, w1_ref, w2_ref)
        colc = lax.broadcasted_iota(jnp.int32, (NSA_H, nb), 1)
        dist_c = past - (colc * CMP_BLOCK + (CMP_BLOCK - 1))
        s_c = _dot_nt(q, kc.astype(BF16)) + _bias_from_dist(dist_c, tab)
        p_c = _masked_softmax(s_c, dist_c >= 0)
        o_c = _dot(p_c.astype(BF16), vc.astype(BF16))

        rg = lax.broadcasted_iota(jnp.int32, (NSA_H, NSA_H), 0) // NSA_GROUP
        cg = lax.broadcasted_iota(jnp.int32, (NSA_H, NSA_H), 1) // NSA_GROUP
        same_group = jnp.where(rg == cg, 1.0, 0.0).astype(BF16)
        score = _dot_exact01(p_c, same_group, 3, left=True)
        cur = past // CMP_BLOCK
        forced = (colc == 0) | (colc == cur) | (colc == cur - 1)
        score = jnp.where(forced, NSA_GROUP + 1.0, jnp.where(colc <= cur, score, -1.0))
        msel = _top_blocks(score, n_top - 1).astype(BF16)

        cols = lax.broadcasted_iota(jnp.int32, (NSA_H, past), 1)
        dist_s = past - cols
        s_s = _dot(q, ks_sc[...]) + _bias_from_dist(dist_s, tab)
        o_s = with_new(s_s, _dot(msel, expand_ref[...]) > 0.5, new[0:1], new[1:2], vs_sc[...])

        wb = win_ref.shape[-1]
        colw = lax.broadcasted_iota(jnp.int32, (NSA_H, wb), 1)
        dist_w = wb - colw
        s_w = _dot(q, win_ref[0, 0, 0].astype(BF16)) + _bias_from_dist(dist_w, tab)
        o_w = with_new(s_w, dist_w <= WINDOW, new[2:3], new[3:4], win_ref[0, 0, 1].astype(BF16))

        gt = _sigmoid(gate_ref[0])
        o = gt[:, 0:1] * o_c + gt[:, 1:2] * o_s + gt[:, 2:3] * o_w
        first_group = lax.broadcasted_iota(jnp.int32, o.shape, 0) < NSA_GROUP
        o_ref[0] = jnp.where(first_group, o, pltpu.roll(o, HEAD_DIM, 1))


def _nsa_decode(page_table, qg, cache_t, win_t, new_rows, gates, tab_t, expand, pe, w1bd, w2bd, li, n_pages):
    bd, npg = page_table.shape
    past = npg * PAGE_SIZE
    nb = past // CMP_BLOCK
    wb = win_t.shape[-1]
    assert npg % n_pages == 0
    full = lambda a: pl.BlockSpec(a.shape, lambda b, p, pt: (0,) * a.ndim)
    per_seq = lambda a: pl.BlockSpec((1,) + a.shape[1:], lambda b, p, pt: (b,) + (0,) * (a.ndim - 1))
    grid_spec = pltpu.PrefetchScalarGridSpec(
        num_scalar_prefetch=1,
        grid=(bd, npg // n_pages),
        in_specs=[per_seq(qg)]
        + _page_specs(n_pages, (1, 1, 4, LANES, PAGE_SIZE), lambda p, g: p * n_pages + g, li)
        + [pl.BlockSpec((1, 1, 2, LANES, wb), lambda b, p, pt: (b, li, 0, 0, 0)),
           per_seq(new_rows), per_seq(gates), full(tab_t), full(expand), full(pe), full(w1bd), full(w2bd)],
        out_specs=pl.BlockSpec((1, NSA_H, LANES), lambda b, p, pt: (b, 0, 0)),
        scratch_shapes=[pltpu.VMEM((past, LANES), F32), pltpu.VMEM((past, LANES), F32),
                        pltpu.VMEM((LANES, past), BF16), pltpu.VMEM((LANES, past), BF16)])
    return pl.pallas_call(
        functools.partial(_nsa_decode_kernel, n_pages=n_pages, n_top=min(TOP_N, nb + 1)),
        grid_spec=grid_spec,
        out_shape=jax.ShapeDtypeStruct((bd, NSA_H, LANES), F32),
        compiler_params=_cparams("parallel", "arbitrary"),
        name="nsa_decode",
    )(page_table, qg, *([cache_t] * n_pages), win_t, new_rows, gates, tab_t, expand, pe, w1bd, w2bd)


def _prep_even_weights(w_in, b_f, cmp_pos, w1, w2):
    w = jnp.concatenate([w_in[:, 0:1280], w_in[:, 1304:1816], w_in[:, 1816:3352], w_in[:, 3360:3872],
                         w_in[:, 1280:1304], w_in[:, 3352:3360],
                         jnp.zeros((D_MODEL, LANES - SMALL_GATE - FOX_H), w_in.dtype)], axis=1).astype(BF16)
    bfp = jnp.zeros((1, LANES), F32).at[0, SMALL_GATE:SMALL_GATE + FOX_H].set(b_f)
    w1r = w1.reshape(2, CMP_BLOCK, HEAD_DIM, HEAD_DIM)
    z1 = jnp.zeros_like(w1r)
    w1bd = jnp.concatenate([jnp.concatenate([w1r, z1], -1), jnp.concatenate([z1, w1r], -1)], -2).astype(BF16)
    pe = jnp.concatenate([cmp_pos, cmp_pos], -1)
    z2 = jnp.zeros_like(w2)
    w2dup = jnp.concatenate([jnp.concatenate([w2, w2, z2, z2], -1), jnp.concatenate([z2, z2, w2, w2], -1)], -2).astype(BF16)
    w2bd = jnp.concatenate([jnp.concatenate([w2, z2], -1), jnp.concatenate([z2, w2], -1)], -2).astype(BF16)
    return w, bfp, pe, w1bd, w2dup, w2bd


def _head_rows(q, n_heads):
    bd = q.shape[0]
    q3 = q.reshape(bd, 1, n_heads, HEAD_DIM)
    eye = jnp.eye(n_heads, dtype=q.dtype).reshape(1, n_heads, n_heads, 1)
    return (q3 * eye).reshape(bd, n_heads, n_heads * HEAD_DIM)


def _group_rows(q):
    bd = q.shape[0]
    q3 = q.reshape(bd, NSA_H, HEAD_DIM)
    z = jnp.zeros_like(q3)
    first = (jnp.arange(NSA_H) < NSA_GROUP).reshape(1, NSA_H, 1)
    return jnp.where(first, jnp.concatenate([q3, z], -1), jnp.concatenate([z, q3], -1))


def kernel(x_prompt, x_sample, cache_nsa, cache_nsa_win, cache_fox, cache_fox_logf, cache_sb, page_table, norm_g, final_g, rel_bias, w_in_even, w_out_even, b_forget, cmp_pos, w_cmp1, w_cmp2, w_in_odd, w_out_odd):
    b, s, d = x_prompt.shape
    bd, ds_, _ = x_sample.shape
    assert ds_ == 1 and d == D_MODEL
    depth = norm_g.shape[0]
    n_even = w_in_even.shape[0]
    n_odd = w_in_odd.shape[0]
    n_phys = cache_nsa.shape[0]
    npg = page_table.shape[1]
    past = npg * PAGE_SIZE
    wb = cache_nsa_win.shape[2]
    t = min(ATT_T, s)
    tm_p = 256
    tm_o = 512

    xp = x_prompt.reshape(b * s, d)
    xs = x_sample.reshape(bd, d)
    gfin = final_g.reshape(1, d)
    rows_last = lambda a: jnp.transpose(a, (0, 1, 3, 4, 5, 2))
    cache_nsa_t = rows_last(cache_nsa).reshape(n_phys, n_even, 4, LANES, PAGE_SIZE)
    win_t = rows_last(cache_nsa_win).reshape(bd, n_even, 2, LANES, wb)
    cache_fox_t = rows_last(cache_fox).reshape(n_phys, n_even, 2, FOX_W, PAGE_SIZE)
    lft = jnp.transpose(cache_fox_logf.astype(F32), (0, 1, 3, 2))
    cache_sb_t = rows_last(cache_sb).reshape(n_phys, n_odd, 2, SB_W, PAGE_SIZE)
    n_pages = math.gcd(npg, DECODE_PAGES)
    tab_t = rel_bias.T
    expand = (jnp.arange(past)[None, :] // CMP_BLOCK == jnp.arange(past // CMP_BLOCK)[:, None]).astype(BF16)

    outs = {k: [] for k in ("nsa_p", "nsa_s", "win_p", "win_s", "fox_p", "fox_s", "lf_p", "lf_s", "sb_p", "sb_s")}
    for layer in range(depth):
        g = norm_g[layer].reshape(1, d)
        li = layer // 2
        final = layer == depth - 1
        if layer % 2 == 0:
            w, bfp, pe, w1bd, w2dup, w2bd = _prep_even_weights(w_in_even[li], b_forget[li], cmp_pos[li], w_cmp1[li],
                                                               w_cmp2[li])
            w_out = w_out_even[li].astype(BF16)
            q, rows, selkv, win, winkv, small, nz, fq, fkv, fk, fv, fz = _even_in(xp, g, w, bfp, tm_p)
            r3 = lambda a: a.reshape(b, s, a.shape[-1])
            cmp = _compress_prompt(r3(rows), pe, w1bd, w2dup)
            o_n = _nsa_prompt(rel_bias, r3(q), r3(selkv), r3(winkv), cmp, r3(small), t)
            logf = r3(small)[:, :, SMALL_GATE:SMALL_GATE + FOX_H]
            c = _cumsum_lanes(jnp.transpose(logf, (0, 2, 1)))
            o_f = _fox_prompt(r3(fq), r3(fk), r3(fv), c.reshape(b, FOX_H // 2, 2, s), t)
            xp = _out_proj(xp, [(o_n.reshape(b * s, NSA_W), nz), (o_f.reshape(b * s, FOX_W), fz)], w_out, gfin, final,
                           tm_o)
            outs["nsa_p"].append(rows.reshape(b, s, 4, NSA_KVH, HEAD_DIM))
            outs["win_p"].append(r3(win)[:, s - min(WINDOW, s):].reshape(b, min(WINDOW, s), 2, NSA_KVH, HEAD_DIM))
            outs["fox_p"].append(fkv.reshape(b, s, 2, FOX_H, HEAD_DIM))
            outs["lf_p"].append(logf)
            q, rows, selkv, win, winkv, small, nz, fq, fkv, fk, fv, fz = _even_in(xs, g, w, bfp, bd)
            new_rows = jnp.stack([rows[:, 2 * LANES:3 * LANES], rows[:, 3 * LANES:], win[:, :LANES], win[:, LANES:]],
                                 axis=1)
            gates = small[:, :SMALL_GATE].reshape(bd, NSA_H, 3)
            o8 = _nsa_decode(page_table, _group_rows(q), cache_nsa_t, win_t, new_rows, gates, tab_t, expand, pe, w1bd,
                             w2bd, li, n_pages)
            o_n = o8[:, :, :HEAD_DIM].reshape(bd, NSA_W)
            logf = small[:, SMALL_GATE:SMALL_GATE + FOX_H]
            o_f = _fox_decode(page_table, _head_rows(fq, FOX_H), cache_fox_t, lft, fkv.reshape(bd, 1, 2 * FOX_W),
                              logf.reshape(bd, FOX_H, 1), li, n_pages).reshape(bd, FOX_W)
            xs = _out_proj(xs, [(o_n, nz), (o_f, fz)], w_out, gfin, final, bd)
            outs["nsa_s"].append(rows.reshape(bd, 1, 4, NSA_KVH, HEAD_DIM))
            outs["win_s"].append(jnp.concatenate([cache_nsa_win[:, li, 1:], win.reshape(bd, 1, 2, NSA_KVH, HEAD_DIM)],
                                                 axis=1))
            outs["fox_s"].append(fkv.reshape(bd, 1, 2, FOX_H, HEAD_DIM))
            outs["lf_s"].append(logf.reshape(bd, 1, FOX_H))
        else:
            w = w_in_odd[li].astype(BF16)
            w_out = w_out_odd[li].astype(BF16)
            q, kv, k, v, z = _odd_in(xp, g, w, tm_p)
            r3 = lambda a: a.reshape(b, s, a.shape[-1])
            o = _sb_prompt(r3(q), r3(k), r3(v), min(SB_T, s))
            xp = _out_proj(xp, [(o.reshape(b * s, SB_W), z)], w_out, gfin, final, tm_o)
            outs["sb_p"].append(kv.reshape(b, s, 2, SB_H, HEAD_DIM))
            q, kv, k, v, z = _odd_in(xs, g, w, bd)
            o = _sb_decode(page_table, _head_rows(q, SB_H), cache_sb_t, li,
                           math.gcd(npg, SB_DECODE_PAGES)).reshape(bd, SB_W)
            xs = _out_proj(xs, [(o, z)], w_out, gfin, final, bd)
            outs["sb_s"].append(kv.reshape(bd, 1, 2, SB_H, HEAD_DIM))

    st = lambda k: jnp.stack(outs[k], 1)
    return (xp.reshape(b, s, d), xs.reshape(bd, 1, d), st("nsa_p"), st("nsa_s"), st("win_p"), st("win_s"),
            st("fox_p"), st("fox_s"), st("lf_p"), st("lf_s"), st("sb_p"), st("sb_s"))
```

```python
import functools
import math

import numpy as np
import jax
import jax.numpy as jnp
from jax import lax
from jax.experimental import pallas as pl
from jax.experimental.pallas import tpu as pltpu

F32 = jnp.float32
BF16 = jnp.bfloat16

D_MODEL = 1024
HEAD_DIM = 64
LANES = 128
NSA_H = 8
NSA_KVH = 2
NSA_GROUP = NSA_H // NSA_KVH
FOX_H = 8
SB_H = 16
NSA_W = NSA_H * HEAD_DIM
FOX_W = FOX_H * HEAD_DIM
SB_W = SB_H * HEAD_DIM
CMP_BLOCK = 64
TOP_N = 16
WINDOW = 512
PAGE_SIZE = 128
N_BUCKETS = 32
REL_MAX_DIST = 128
RMS_EPS = 1e-6
NEG_INF = -1e30
SCALE = HEAD_DIM ** -0.5
EXP_ZERO = 104.0

ATT_T = 256
SB_T = 256
DECODE_PAGES = 16
SB_DECODE_PAGES = 2
VMEM_LIMIT = 56 * 1024 * 1024


def _cparams(*sem):
    return pltpu.CompilerParams(dimension_semantics=sem, vmem_limit_bytes=VMEM_LIMIT)


def _dot(a, b):
    return jnp.dot(a, b, preferred_element_type=F32)


def _dot_nt(a, b):
    return lax.dot_general(a, b, (((1,), (1,)), ((), ())), preferred_element_type=F32)


def _split2(x):
    hi = x.astype(BF16)
    lo = (x - hi.astype(F32)).astype(BF16)
    return hi, lo


def _dot_exact01(x, w01, passes, left=False):
    acc = None
    r = x
    for _ in range(passes):
        piece = r.astype(BF16)
        term = _dot(w01, piece) if left else _dot(piece, w01)
        acc = term if acc is None else acc + term
        r = r - piece.astype(F32)
    return acc


def _softplus(z):
    return jnp.maximum(z, 0.0) + jnp.log1p(jnp.exp(-jnp.abs(z)))


def _sigmoid(z):
    return 1.0 / (1.0 + jnp.exp(-z))


def _rms(x, g):
    return x * lax.rsqrt(jnp.mean(x * x, axis=-1, keepdims=True) + RMS_EPS) * g


def _lane_lo(shape):
    return lax.broadcasted_iota(jnp.int32, shape, len(shape) - 1) < HEAD_DIM


def _dup_halves(x):
    r = pltpu.roll(x, HEAD_DIM, 1)
    lo = _lane_lo(x.shape)
    return jnp.where(lo, x, r), jnp.where(lo, r, x)


E_Q, E_ROWS, E_WIN, E_NZ, E_FQ, E_FK, E_FV, E_FZ, E_SMALL, E_END = (
    0, 512, 1024, 1280, 1792, 2304, 2816, 3328, 3840, 3968)
SMALL_GATE = 3 * NSA_H


def _even_in_kernel(*refs, stacked, aliased, layer):
    x_ref, g_ref, w_ref, bf_ref = refs[:4]
    n_in = 4 + (2 if stacked else 0) + (2 if aliased else 0)
    (q_ref, rows_ref, selkv_ref, win_ref, winkv_ref, small_ref, nz_ref, fq_ref, fkv_ref, fk_ref, fv_ref,
     fz_ref) = refs[n_in:n_in + 12]
    xn = _rms(x_ref[...], g_ref[...]).astype(BF16)

    def mm(a, b):
        return _dot(xn, w_ref[:, a:b])

    q_ref[...] = (mm(E_Q, E_ROWS) * SCALE).astype(BF16)
    rows = mm(E_ROWS, E_WIN)
    rows_ref[...] = rows
    kd0, kd1 = _dup_halves(rows[:, 256:384])
    vd0, vd1 = _dup_halves(rows[:, 384:512])
    selkv_ref[...] = jnp.concatenate([kd0, kd1, vd0, vd1], axis=1).astype(BF16)
    win = mm(E_WIN, E_NZ)
    win_ref[...] = win
    kd0, kd1 = _dup_halves(win[:, 0:128])
    vd0, vd1 = _dup_halves(win[:, 128:256])
    winkv_ref[...] = jnp.concatenate([kd0, kd1, vd0, vd1], axis=1).astype(BF16)
    nz_ref[...] = mm(E_NZ, E_FQ)
    fq_ref[...] = (mm(E_FQ, E_FK) * SCALE).astype(BF16)
    fkv = mm(E_FK, E_FZ)
    fk_ref[...] = fkv[:, :FOX_W].astype(BF16)
    fv_ref[...] = fkv[:, FOX_W:].astype(BF16)
    if stacked:
        rows_t_ref = refs[n_in + 12]
        _store_layer(rows_t_ref, _dot_nt(refs[4][...], xn), layer, aliased)
        _store_layer(fkv_ref, _dot_nt(refs[5][...], xn), layer, aliased)
    else:
        fkv_ref[...] = fkv
    fz_ref[...] = mm(E_FZ, E_SMALL)
    small = mm(E_SMALL, E_END)
    lane = lax.broadcasted_iota(jnp.int32, small.shape, 1)
    is_f = (lane >= SMALL_GATE) & (lane < SMALL_GATE + FOX_H)
    small_ref[...] = jnp.where(is_f, -_softplus(-(small + bf_ref[...])), small)


def _stacked_spec(cols, tm, tiles_per_batch, li, n_layers, aliased):
    if aliased:
        return pl.BlockSpec((1, 1, cols, tm), lambda i: (i // tiles_per_batch, li, 0, i % tiles_per_batch))
    return pl.BlockSpec((1, n_layers, cols, tm), lambda i: (i // tiles_per_batch, 0, 0, i % tiles_per_batch))


def _store_layer(ref, data, li, aliased):
    if aliased:
        ref[0, 0] = data
    else:
        for layer in range(ref.shape[1]):
            ref[0, layer] = data if layer == li else jnp.zeros_like(data)


def _even_in(x, g, w, bfp, tm, stack=None):
    m = x.shape[0]
    row = lambda c: pl.BlockSpec((tm, c), lambda i: (i, 0))
    full = lambda a: pl.BlockSpec(a.shape, lambda i: (0, 0))
    outs = [(512, BF16), (512, F32), (512, BF16), (256, F32), (512, BF16), (128, F32),
            (512, F32), (512, BF16), (1024, F32), (512, BF16), (512, BF16), (512, F32)]
    in_specs = [row(D_MODEL), full(g), full(w), full(bfp)]
    args = [x, g, w, bfp]
    out_specs = [row(c) for c, _ in outs]
    out_shape = [jax.ShapeDtypeStruct((m, c), dt) for c, dt in outs]
    aliases = {}
    if stack is not None:
        li, n_layers, b, wt_rows, wt_fkv, prev = stack
        s = m // b
        tpb = s // tm
        in_specs += [full(wt_rows), full(wt_fkv)]
        args += [wt_rows, wt_fkv]
        out_specs[8] = _stacked_spec(2 * FOX_W, tm, tpb, li, n_layers, prev is not None)
        out_shape[8] = jax.ShapeDtypeStruct((b, n_layers, 2 * FOX_W, s), F32)
        out_specs.append(_stacked_spec(4 * LANES, tm, tpb, li, n_layers, prev is not None))
        out_shape.append(jax.ShapeDtypeStruct((b, n_layers, 4 * LANES, s), F32))
        if prev is not None:
            in_specs += [pl.BlockSpec(memory_space=pl.ANY)] * 2
            args += [prev[0], prev[1]]
            aliases = {6: 12, 7: 8}
    return pl.pallas_call(
        functools.partial(_even_in_kernel, stacked=stack is not None, aliased=bool(aliases),
                          layer=None if stack is None else stack[0]),
        grid=(m // tm,),
        in_specs=in_specs,
        out_specs=out_specs,
        out_shape=out_shape,
        input_output_aliases=aliases,
        compiler_params=_cparams("parallel"),
        name="even_in",
    )(*args)


def _odd_in_kernel(*refs, stacked, aliased, layer):
    x_ref, g_ref, w_ref = refs[:3]
    n_in = 3 + (1 if stacked else 0) + (1 if aliased else 0)
    q_ref, kv_ref, k_ref, v_ref, z_ref = refs[n_in:]
    xn = _rms(x_ref[...], g_ref[...]).astype(BF16)
    q_ref[...] = (_dot(xn, w_ref[:, 0:SB_W]) * SCALE).astype(BF16)
    kv = _dot(xn, w_ref[:, SB_W:3 * SB_W])
    if stacked:
        _store_layer(kv_ref, _dot_nt(refs[3][...], xn), layer, aliased)
    else:
        kv_ref[...] = kv
    k_ref[...] = kv[:, :SB_W].astype(BF16)
    v_ref[...] = kv[:, SB_W:].astype(BF16)
    z_ref[...] = _dot(xn, w_ref[:, 3 * SB_W:])


def _odd_in(x, g, w, tm, stack=None):
    m = x.shape[0]
    row = lambda c: pl.BlockSpec((tm, c), lambda i: (i, 0))
    full = lambda a: pl.BlockSpec(a.shape, lambda i: (0, 0))
    outs = [(SB_W, BF16), (2 * SB_W, F32), (SB_W, BF16), (SB_W, BF16), (SB_W, F32)]
    in_specs = [row(D_MODEL), full(g), full(w)]
    args = [x, g, w]
    out_specs = [row(c) for c, _ in outs]
    out_shape = [jax.ShapeDtypeStruct((m, c), dt) for c, dt in outs]
    aliases = {}
    if stack is not None:
        li, n_layers, b, wt_kv, prev = stack
        s = m // b
        in_specs.append(full(wt_kv))
        args.append(wt_kv)
        out_specs[1] = _stacked_spec(2 * SB_W, tm, s // tm, li, n_layers, prev is not None)
        out_shape[1] = jax.ShapeDtypeStruct((b, n_layers, 2 * SB_W, s), F32)
        if prev is not None:
            in_specs.append(pl.BlockSpec(memory_space=pl.ANY))
            args.append(prev)
            aliases = {4: 1}
    return pl.pallas_call(
        functools.partial(_odd_in_kernel, stacked=stack is not None, aliased=bool(aliases),
                          layer=None if stack is None else stack[0]),
        grid=(m // tm,),
        in_specs=in_specs,
        out_specs=out_specs,
        out_shape=out_shape,
        input_output_aliases=aliases,
        compiler_params=_cparams("parallel"),
        name="odd_in",
    )(*args)


def _out_kernel(n_parts, final, *refs):
    x_ref = refs[0]
    parts = refs[1:1 + 2 * n_parts]
    w_ref = refs[1 + 2 * n_parts]
    gf_ref = refs[2 + 2 * n_parts]
    y_ref = refs[3 + 2 * n_parts]
    y = x_ref[...]
    off = 0
    for p in range(n_parts):
        o = parts[2 * p][...]
        z = parts[2 * p + 1][...]
        k = o.shape[1]
        y = y + _dot((o * (z * _sigmoid(z))).astype(BF16), w_ref[off:off + k, :])
        off += k
    if final:
        y = _rms(y, gf_ref[...])
    y_ref[...] = y


def _out_proj(x, parts, w, gf, final, tm):
    m = x.shape[0]
    row = lambda c: pl.BlockSpec((tm, c), lambda i: (i, 0))
    full = lambda a: pl.BlockSpec(a.shape, lambda i: (0, 0))
    flat = [a for pr in parts for a in pr]
    return pl.pallas_call(
        functools.partial(_out_kernel, len(parts), final),
        grid=(m // tm,),
        in_specs=[row(D_MODEL)] + [row(a.shape[1]) for a in flat] + [full(w), full(gf)],
        out_specs=row(D_MODEL),
        out_shape=jax.ShapeDtypeStruct((m, D_MODEL), F32),
        compiler_params=_cparams("parallel"),
        name="out_proj",
    )(x, *flat, w, gf)


def _cumsum_kernel(x_ref, o_ref, *, chunk):
    n = x_ref.shape[2] // chunk
    r = lax.broadcasted_iota(jnp.int32, (chunk, chunk), 0)
    c = lax.broadcasted_iota(jnp.int32, (chunk, chunk), 1)
    tri = jnp.where(r <= c, 1.0, 0.0).astype(BF16)

    def body(j, carry):
        x = x_ref[0, :, pl.ds(j * chunk, chunk)]
        cs = _dot_exact01(x, tri, 3) + carry
        o_ref[0, :, pl.ds(j * chunk, chunk)] = cs
        return cs[:, chunk - 1:chunk]

    lax.fori_loop(0, n, body, jnp.zeros((x_ref.shape[1], 1), F32))


def _cumsum_lanes(x):
    b, h, s = x.shape
    spec = pl.BlockSpec((1, h, s), lambda i: (i, 0, 0))
    return pl.pallas_call(
        functools.partial(_cumsum_kernel, chunk=min(256, s)),
        grid=(b,),
        in_specs=[spec],
        out_specs=spec,
        out_shape=jax.ShapeDtypeStruct(x.shape, F32),
        compiler_params=_cparams("parallel"),
        name="logf_cumsum",
    )(x)


def _fox_prompt_kernel(q_ref, k_ref, v_ref, c_ref, o_ref, kmax_sc, *, t):
    i = pl.program_id(2)
    q2 = q_ref[0]
    lo = _lane_lo(q2.shape)
    zero = jnp.zeros_like(q2)
    qs = (jnp.where(lo, q2, zero), jnp.where(lo, zero, q2))
    row = lax.broadcasted_iota(jnp.int32, (t, t), 0)
    col = lax.broadcasted_iota(jnp.int32, (t, t), 1)
    causal = row >= col

    @pl.when(i == 0)
    def _():
        def norm_step(cidx, mx):
            kk = k_ref[0, pl.ds(cidx * t, t), :].astype(F32)
            kk = kk * kk
            n0 = jnp.max(jnp.sum(jnp.where(lo, kk, 0.0), axis=1, keepdims=True), axis=0, keepdims=True)
            n1 = jnp.max(jnp.sum(jnp.where(lo, 0.0, kk), axis=1, keepdims=True), axis=0, keepdims=True)
            return jnp.maximum(mx[0], n0), jnp.maximum(mx[1], n1)

        z11 = jnp.zeros((1, 1), F32)
        mx = lax.fori_loop(0, k_ref.shape[1] // t, norm_step, (z11, z11))
        for h in range(2):
            kmax_sc[h] = jnp.broadcast_to(jnp.sqrt(mx[h]), kmax_sc.shape[1:])

    qf = q2.astype(F32)
    qq = qf * qf
    bound = []
    for h in range(2):
        qn = jnp.sqrt(jnp.sum(jnp.where(lo, qq, 0.0) if h == 0 else jnp.where(lo, 0.0, qq), axis=1, keepdims=True))
        bound.append(qn * kmax_sc[h][0:1, 0:1] + 1.0)

    def step(j, carry, masked):
        kblk = k_ref[0, pl.ds(j * t, t), :]
        vblk = v_ref[0, pl.ds(j * t, t), :]
        cb = c_ref[0, 0, :, pl.ds(j * t, t)]
        new = []
        for h in range(2):
            m_old, l_old, acc = carry[h]
            s = _dot_nt(qs[h], kblk) - cb[h:h + 1, :]
            if masked:
                s = jnp.where(causal, s, NEG_INF)
            m_new = jnp.maximum(m_old, jnp.max(s, axis=1, keepdims=True))
            alpha = jnp.exp(m_old - m_new)
            p = jnp.exp(s - m_new)
            if masked:
                p = jnp.where(causal, p, 0.0)
            l_new = alpha * l_old + jnp.sum(p, axis=1, keepdims=True)
            acc = alpha * acc + _dot(p.astype(BF16), vblk)
            new.append((m_new, l_new, acc))
        return tuple(new)

    def older_tiles_matter(jj, carry):
        cend = c_ref[0, 0, :, pl.ds(jnp.maximum(jj, 0) * t, t)][:, t - 1:t]
        gap = [jnp.max(bound[h] - carry[h][0], axis=0, keepdims=True) - cend[h:h + 1, :] for h in range(2)]
        return (jnp.max(jnp.maximum(gap[0], gap[1])) >= -EXP_ZERO).astype(jnp.int32)

    def body(state):
        jj, _, carry = state
        carry = step(jj, carry, False)
        return jj - 1, older_tiles_matter(jj - 1, carry), carry

    init = tuple((jnp.full((t, 1), NEG_INF, F32), jnp.zeros((t, 1), F32), jnp.zeros((t, LANES), F32))
                 for _ in range(2))
    carry = step(i, init, True)
    _, _, carry = lax.while_loop(lambda st: (st[0] >= 0) & (st[1] > 0), body,
                                 (i - 1, older_tiles_matter(i - 1, carry), carry))
    outs = [acc / jnp.maximum(l, 1e-30) for (_, l, acc) in carry]
    o_ref[0] = jnp.where(lo, outs[0], outs[1])


def _fox_prompt(q, k, v, c4, t):
    b, s, w = q.shape
    hp = w // LANES
    return pl.pallas_call(
        functools.partial(_fox_prompt_kernel, t=t),
        grid=(b, hp, s // t),
        in_specs=[pl.BlockSpec((1, t, LANES), lambda bi, h, i: (bi, i, h)),
                  pl.BlockSpec((1, s, LANES), lambda bi, h, i: (bi, 0, h)),
                  pl.BlockSpec((1, s, LANES), lambda bi, h, i: (bi, 0, h)),
                  pl.BlockSpec((1, 1, 2, s), lambda bi, h, i: (bi, h, 0, 0))],
        out_specs=pl.BlockSpec((1, t, LANES), lambda bi, h, i: (bi, i, h)),
        out_shape=jax.ShapeDtypeStruct((b, s, w), F32),
        scratch_shapes=[pltpu.VMEM((2, 8, LANES), F32)],
        compiler_params=_cparams("parallel", "parallel", "arbitrary"),
        name="fox_prompt",
    )(q, k, v, c4)


def _sb_weights(z, after_carry, ustrict, mask):
    sp = _softplus(z)
    l1m = -sp if mask is None else jnp.where(mask, -sp, 0.0)
    after = _dot_exact01(l1m, ustrict, 2) + after_carry
    a = jnp.exp(z - sp + after)
    if mask is not None:
        a = jnp.where(mask, a, 0.0)
    return a, after_carry + jnp.sum(l1m, axis=1, keepdims=True)


def _sb_prompt_kernel(q_ref, k_ref, v_ref, o_ref, *, t):
    i = pl.program_id(2)
    q2 = q_ref[0]
    lo = _lane_lo(q2.shape)
    zero = jnp.zeros_like(q2)
    qs = (jnp.where(lo, q2, zero), jnp.where(lo, zero, q2))
    row = lax.broadcasted_iota(jnp.int32, (t, t), 0)
    col = lax.broadcasted_iota(jnp.int32, (t, t), 1)
    strict = row > col
    ustrict = jnp.where(strict, 1.0, 0.0).astype(BF16)

    def step(j, carry, mask):
        kblk = k_ref[0, pl.ds(j * t, t), :]
        vblk = v_ref[0, pl.ds(j * t, t), :]
        new = []
        for h in range(2):
            run, acc = carry[h]
            a, run = _sb_weights(_dot_nt(qs[h], kblk), run, ustrict, mask)
            new.append((run, acc + _dot(a.astype(BF16), vblk)))
        return tuple(new)

    def older_tiles_matter(carry):
        return (jnp.max(jnp.maximum(carry[0][0], carry[1][0])) >= -EXP_ZERO).astype(jnp.int32)

    def body(state):
        jj, _, carry = state
        carry = step(jj, carry, None)
        return jj - 1, older_tiles_matter(carry), carry

    init = tuple((jnp.zeros((t, 1), F32), jnp.zeros((t, LANES), F32)) for _ in range(2))
    carry = step(i, init, strict)
    _, _, carry = lax.while_loop(lambda st: (st[0] >= 0) & (st[1] > 0), body,
                                 (i - 1, older_tiles_matter(carry), carry))
    o_ref[0] = jnp.where(lo, carry[0][1], carry[1][1])


def _sb_prompt(q, k, v, t):
    b, s, w = q.shape
    hp = w // LANES
    return pl.pallas_call(
        functools.partial(_sb_prompt_kernel, t=t),
        grid=(b, hp, s // t),
        in_specs=[pl.BlockSpec((1, t, LANES), lambda bi, h, i: (bi, i, h)),
                  pl.BlockSpec((1, s, LANES), lambda bi, h, i: (bi, 0, h)),
                  pl.BlockSpec((1, s, LANES), lambda bi, h, i: (bi, 0, h))],
        out_specs=pl.BlockSpec((1, t, LANES), lambda bi, h, i: (bi, i, h)),
        out_shape=jax.ShapeDtypeStruct((b, s, w), F32),
        compiler_params=_cparams("parallel", "parallel", "arbitrary"),
        name="sb_prompt",
    )(q, k, v)


def _bucket_starts():
    d = np.arange(0, 4 * REL_MAX_DIST)
    max_exact = N_BUCKETS // 2
    large = max_exact + (np.log(np.maximum(d, 1).astype(np.float32) / max_exact)
                         / math.log(REL_MAX_DIST / max_exact) * (N_BUCKETS - max_exact)).astype(np.int32)
    b = np.where(d < max_exact, d, np.minimum(large, N_BUCKETS - 1))
    return [int(np.min(np.nonzero(b >= k)[0])) for k in range(N_BUCKETS)]


BUCKET_STARTS = _bucket_starts()
BIAS_FAR_DIST = BUCKET_STARTS[-1]


def _bias_from_dist(dist, tab):
    v = jnp.where(dist >= BUCKET_STARTS[1], tab(1), tab(0))
    for b in range(2, N_BUCKETS):
        v = jnp.where(dist >= BUCKET_STARTS[b], tab(b), v)
    return v


def _masked_softmax(s, mask):
    logits = jnp.where(mask, s, NEG_INF)
    m = jnp.max(logits, axis=-1, keepdims=True)
    e = jnp.where(mask, jnp.exp(logits - m), 0.0)
    return e / jnp.maximum(jnp.sum(e, axis=-1, keepdims=True), 1e-30)


def _top_blocks(score, n_top):
    lanef = lax.broadcasted_iota(jnp.int32, score.shape, 1).astype(F32)
    work = score
    msel = jnp.zeros(score.shape, F32)
    for _ in range(n_top):
        mx = jnp.max(work, axis=1, keepdims=True)
        idx = jnp.min(jnp.where(work == mx, lanef, 1e9), axis=1, keepdims=True)
        pick = lanef == idx
        msel = jnp.where(pick, 1.0, msel)
        work = jnp.where(pick, -2.0, work)
    return msel


def _compress_rows(k_ref, v_ref, nb, pe_ref, w1_ref, w2_ref):
    outs = []
    for kv, ref in enumerate((k_ref, v_ref)):
        acc = jnp.zeros((nb, LANES), F32)
        for pos in range(CMP_BLOCK):
            x = ref[pl.ds(pos, nb, stride=CMP_BLOCK), :] + pe_ref[kv, pos:pos + 1, :]
            acc = acc + _dot(x.astype(BF16), w1_ref[kv, pos])
        hid = acc * _sigmoid(acc)
        outs.append(_dot(hid.astype(BF16), w2_ref[kv]))
    return outs


def _compress_kernel(k_ref, v_ref, pe_ref, w1_ref, w2_ref, o_ref):
    nb = o_ref.shape[1]
    kc, vc = _compress_rows(k_ref.at[0], v_ref.at[0], nb, pe_ref, w1_ref, w2_ref)
    o_ref[0] = jnp.concatenate([kc, vc], axis=1).astype(BF16)


def _compress_prompt(rows, pe, w1bd, w2big):
    b, s, _ = rows.shape
    nb = s // CMP_BLOCK
    full = lambda a: pl.BlockSpec(a.shape, lambda i: (0,) * a.ndim)
    return pl.pallas_call(
        _compress_kernel,
        grid=(b,),
        in_specs=[pl.BlockSpec((1, s, LANES), lambda i: (i, 0, 0)), pl.BlockSpec((1, s, LANES), lambda i: (i, 0, 1)),
                  full(pe), full(w1bd), full(w2big)],
        out_specs=pl.BlockSpec((1, nb, 4 * LANES), lambda i: (i, 0, 0)),
        out_shape=jax.ShapeDtypeStruct((b, nb, 4 * LANES), BF16),
        compiler_params=_cparams("parallel"),
        name="nsa_compress_prompt",
    )(rows, rows, pe, w1bd, w2big)


def _nsa_prompt_kernel(tab_ref, q_ref, ks_ref, vs_ref, kw_ref, vw_ref, kc_ref, vc_ref, gate_ref, o_ref, tiles_sc,
                       cmpb_sc, *, t, n_top):
    g = pl.program_id(1)
    i = pl.program_id(2)
    nh = NSA_GROUP
    nb = kc_ref.shape[1]
    bpt = t // CMP_BLOCK
    row = lax.broadcasted_iota(jnp.int32, (t, t), 0)
    col = lax.broadcasted_iota(jnp.int32, (t, t), 1)
    rowc = lax.broadcasted_iota(jnp.int32, (t, nb), 0)
    colc = lax.broadcasted_iota(jnp.int32, (t, nb), 1)

    @pl.when(i == 0)
    def _():
        for hh in range(nh):
            h = g * nh + hh
            tab = lambda b: tab_ref[b, h]
            far = tab_ref[N_BUCKETS - 1, h]
            for kind in range(2):
                tiles_sc[kind, hh] = _bias_from_dist(kind * t + row - col, tab) - far
            rel = jnp.where(colc < nb // 2, colc, colc - nb)
            dist0 = rowc - (rel * CMP_BLOCK + (CMP_BLOCK - 1))
            cmpb_sc[hh] = jnp.where(dist0 >= 0, _bias_from_dist(dist0, tab) - far, 0.0)

    q4 = q_ref[0]
    lo = _lane_lo((t, LANES))
    zero = jnp.zeros((t, LANES), BF16)
    qa, qb = q4[:, :LANES], q4[:, LANES:]
    qst = jnp.concatenate([jnp.where(lo, qa, zero), jnp.where(lo, zero, qa),
                           jnp.where(lo, qb, zero), jnp.where(lo, zero, qb)], axis=0)

    dist_c = (i * t + rowc) - (colc * CMP_BLOCK + (CMP_BLOCK - 1))
    bias_c = jnp.stack([pltpu.roll(cmpb_sc[hh], bpt * i, 1) for hh in range(nh)], axis=0)
    valid_c = (dist_c >= 0)[None]
    p_c = _masked_softmax(_dot_nt(qst, kc_ref[0]).reshape(nh, t, nb) + bias_c, valid_c)
    o_c = _dot(p_c.reshape(nh * t, nb).astype(BF16), vc_ref[0])

    cur = (i * t + rowc) // CMP_BLOCK
    forced = (colc == 0) | (colc == cur) | (colc == cur - 1)
    score = jnp.where(forced, NSA_GROUP + 1.0, jnp.where(colc <= cur, jnp.sum(p_c, axis=0), -1.0))
    msel = _top_blocks(score, n_top).astype(BF16)

    blk_of_key = lax.broadcasted_iota(jnp.int32, (nb, t), 0)
    key_in_tile = lax.broadcasted_iota(jnp.int32, (nb, t), 1)

    def flash_step(carry, j, k_ref, v_ref, mask, bias):
        m_old, l_old, acc = carry
        kblk = k_ref[0, pl.ds(j * t, t), :]
        vblk = v_ref[0, pl.ds(j * t, t), :]
        s = _dot_nt(qst, kblk).reshape(nh, t, t)
        if bias is not None:
            s = s + bias
        s = jnp.where(mask[None], s, NEG_INF)
        m_new = jnp.maximum(m_old, jnp.max(s, axis=-1, keepdims=True))
        alpha = jnp.exp(m_old - m_new)
        p = jnp.exp(s - m_new)
        l_new = alpha * l_old + jnp.sum(p, axis=-1, keepdims=True)
        acc = alpha.reshape(nh * t, 1) * acc + _dot(p.reshape(nh * t, t).astype(BF16), vblk)
        return m_new, l_new, acc

    def finish(carry):
        _, l, acc = carry
        return acc / jnp.maximum(l, 1e-30).reshape(nh * t, 1)

    def selected(j, on):
        expand = jnp.where(blk_of_key == jnp.right_shift(j * t + key_in_tile, 6), 1.0, 0.0).astype(BF16)
        return _dot(msel, expand) > jnp.where(on, 0.5, 2.0)

    dist0 = row - col
    prev = jnp.maximum(i - 1, 0)
    init = (jnp.full((nh, t, 1), NEG_INF, F32), jnp.zeros((nh, t, 1), F32), jnp.zeros((nh * t, LANES), F32))
    carry = flash_step(init, i, ks_ref, vs_ref, selected(i, True) & (dist0 >= 0), tiles_sc[0])
    carry = flash_step(carry, prev, ks_ref, vs_ref, selected(prev, i >= 1), tiles_sc[1])
    o_s = finish(lax.fori_loop(0, prev, lambda j, c: flash_step(c, j, ks_ref, vs_ref, selected(j, True), None), carry))

    carry = flash_step(init, i, kw_ref, vw_ref, (dist0 >= 0) & (dist0 <= WINDOW), tiles_sc[0])
    for k in range(1, (WINDOW + t - 1) // t + 1):
        reach = jnp.where(i >= k, WINDOW - k * t, -t - 1)
        carry = flash_step(carry, jnp.maximum(i - k, 0), kw_ref, vw_ref, dist0 <= reach,
                           tiles_sc[1] if k == 1 else None)
    o_w = finish(carry)

    sig = _sigmoid(gate_ref[0])
    lane = lax.broadcasted_iota(jnp.int32, sig.shape, 1)

    def gate(hh, br):
        return jnp.sum(jnp.where(lane == (g * nh + hh) * 3 + br, sig, 0.0), axis=1, keepdims=True)

    outs = []
    for hh in range(nh):
        sl = slice(hh * t, (hh + 1) * t)
        outs.append(gate(hh, 0) * o_c[sl] + gate(hh, 1) * o_s[sl] + gate(hh, 2) * o_w[sl])
    o_ref[0] = jnp.concatenate([jnp.where(lo, outs[0], outs[1]), jnp.where(lo, outs[2], outs[3])], axis=1)


def _nsa_prompt(tab, q, selkv, winkv, cmp, small, t):
    b, s, _ = q.shape
    nb = cmp.shape[1]
    assert t >= BIAS_FAR_DIST and s % t == 0
    kv = lambda off: pl.BlockSpec((1, s, LANES), lambda bi, g, i: (bi, 0, off + g))
    cm = lambda off: pl.BlockSpec((1, nb, LANES), lambda bi, g, i: (bi, 0, off + g))
    return pl.pallas_call(
        functools.partial(_nsa_prompt_kernel, t=t, n_top=min(TOP_N, nb)),
        grid=(b, NSA_KVH, s // t),
        in_specs=[pl.BlockSpec(memory_space=pltpu.SMEM),
                  pl.BlockSpec((1, t, 2 * LANES), lambda bi, g, i: (bi, i, g)),
                  kv(0), kv(2), kv(0), kv(2), cm(0), cm(2),
                  pl.BlockSpec((1, t, LANES), lambda bi, g, i: (bi, i, 0))],
        out_specs=pl.BlockSpec((1, t, 2 * LANES), lambda bi, g, i: (bi, i, g)),
        out_shape=jax.ShapeDtypeStruct((b, s, NSA_W), F32),
        scratch_shapes=[pltpu.VMEM((2, NSA_GROUP, t, t), F32), pltpu.VMEM((NSA_GROUP, t, nb), F32)],
        compiler_params=_cparams("parallel", "parallel", "arbitrary"),
        name="nsa_prompt",
    )(tab, q, selkv, selkv, winkv, winkv, cmp, cmp, small)


def _head_diag(x, width):
    r = lax.broadcasted_iota(jnp.int32, x.shape, 0)
    c = lax.broadcasted_iota(jnp.int32, x.shape, 1)
    own = (c >= r * width) & (c < (r + 1) * width)
    return jnp.sum(jnp.where(own, x, 0.0), axis=0, keepdims=True)


def _page_specs(n, block, page_of, li):
    return [pl.BlockSpec(block, functools.partial(
        lambda b, p, pt, g: (pt[b, page_of(p, g)], li) + (0,) * (len(block) - 2), g=g)) for g in range(n)]


def _fox_decode_kernel(pt_ref, q_ref, *refs, n_pages):
    page_refs = refs[:n_pages]
    lf_refs = refs[n_pages:2 * n_pages]
    kvn_ref, lfn_ref, o_ref, m_sc, l_sc, acc_sc, c_sc = refs[2 * n_pages:]
    p = pl.program_id(1)
    w = FOX_W

    @pl.when(p == 0)
    def _():
        m_sc[...] = jnp.full(m_sc.shape, NEG_INF, F32)
        l_sc[...] = jnp.zeros(l_sc.shape, F32)
        acc_sc[...] = jnp.zeros(acc_sc.shape, F32)
        c_sc[...] = jnp.zeros(c_sc.shape, F32)

    q = q_ref[0]
    r = lax.broadcasted_iota(jnp.int32, (PAGE_SIZE, PAGE_SIZE), 0)
    cidx = lax.broadcasted_iota(jnp.int32, (PAGE_SIZE, PAGE_SIZE), 1)
    tri = jnp.where(r <= cidx, 1.0, 0.0).astype(BF16)
    m, l, acc, c_run = m_sc[...], l_sc[...], acc_sc[...], c_sc[...]
    scores = []
    for page_ref, lf_ref in zip(page_refs, lf_refs):
        lf = lf_ref[0, 0]
        c = _dot_exact01(lf, tri, 3) + c_run
        c_run = c_run + jnp.sum(lf, axis=1, keepdims=True)
        scores.append(_dot(q, page_ref[0, 0, 0].astype(BF16)) - c)
    s = jnp.concatenate(scores, axis=1)
    m_new = jnp.maximum(m, jnp.max(s, axis=1, keepdims=True))
    alpha = jnp.exp(m - m_new)
    e = jnp.exp(s - m_new)
    l = alpha * l + jnp.sum(e, axis=1, keepdims=True)
    acc = alpha * acc
    for g, page_ref in enumerate(page_refs):
        acc = acc + _dot_nt(e[:, g * PAGE_SIZE:(g + 1) * PAGE_SIZE].astype(BF16), page_ref[0, 0, 1].astype(BF16))
    m = m_new
    m_sc[...], l_sc[...], acc_sc[...], c_sc[...] = m, l, acc, c_run

    @pl.when(p == pl.num_programs(1) - 1)
    def _():
        kvn = kvn_ref[0].astype(BF16).astype(F32)
        s_n = jnp.sum(q.astype(F32) * kvn[:, :w], axis=1, keepdims=True) - (c_run + lfn_ref[0])
        m_fin = jnp.maximum(m, s_n)
        a = jnp.exp(m - m_fin)
        e_n = jnp.exp(s_n - m_fin)
        l_fin = a * l + e_n
        out = a * acc + e_n.astype(BF16).astype(F32) * kvn[:, w:]
        o_ref[0] = _head_diag(out / jnp.maximum(l_fin, 1e-30), HEAD_DIM)


def _fox_decode(page_table, qrows, cache_t, lft, kv_new, lf_new, li, n_pages):
    bd, npg = page_table.shape
    h = FOX_H
    assert npg % n_pages == 0
    page_of = lambda p, g: p * n_pages + g
    grid_spec = pltpu.PrefetchScalarGridSpec(
        num_scalar_prefetch=1,
        grid=(bd, npg // n_pages),
        in_specs=[pl.BlockSpec((1, h, FOX_W), lambda b, p, pt: (b, 0, 0))]
        + _page_specs(n_pages, (1, 1, 2, FOX_W, PAGE_SIZE), page_of, li)
        + _page_specs(n_pages, (1, 1, h, PAGE_SIZE), page_of, li)
        + [pl.BlockSpec((1, 1, 2 * FOX_W), lambda b, p, pt: (b, 0, 0)),
           pl.BlockSpec((1, h, 1), lambda b, p, pt: (b, 0, 0))],
        out_specs=pl.BlockSpec((1, 1, FOX_W), lambda b, p, pt: (b, 0, 0)),
        scratch_shapes=[pltpu.VMEM((h, 1), F32), pltpu.VMEM((h, 1), F32), pltpu.VMEM((h, FOX_W), F32),
                        pltpu.VMEM((h, 1), F32)])
    return pl.pallas_call(
        functools.partial(_fox_decode_kernel, n_pages=n_pages),
        grid_spec=grid_spec,
        out_shape=jax.ShapeDtypeStruct((bd, 1, FOX_W), F32),
        compiler_params=_cparams("parallel", "arbitrary"),
        name="fox_decode",
    )(page_table, qrows, *([cache_t] * n_pages), *([lft] * n_pages), kv_new, lf_new)


def _sb_decode_kernel(pt_ref, q_ref, cache_ref, o_ref, buf, sem, *, li, n_pages):
    b = pl.program_id(0)
    npg = pt_ref.shape[1]
    n_chunks = npg // n_pages

    def page_copy(c, g, slot):
        page = pt_ref[b, npg - 1 - (c * n_pages + g)]
        return pltpu.make_async_copy(cache_ref.at[page, li], buf.at[slot, g], sem.at[slot])

    def start(c, slot):
        for g in range(n_pages):
            page_copy(c, g, slot).start()

    def wait(c, slot):
        for g in range(n_pages):
            page_copy(c, g, slot).wait()

    r = lax.broadcasted_iota(jnp.int32, (PAGE_SIZE, PAGE_SIZE), 0)
    cidx = lax.broadcasted_iota(jnp.int32, (PAGE_SIZE, PAGE_SIZE), 1)
    ustrict = jnp.where(r > cidx, 1.0, 0.0).astype(BF16)
    q = q_ref[0]

    def body(state):
        c, _, run, acc = state
        slot = lax.rem(c, 2)

        @pl.when(c + 1 < n_chunks)
        def _():
            start(c + 1, 1 - slot)

        wait(c, slot)
        for g in range(n_pages):
            a, run = _sb_weights(_dot(q, buf[slot, g, 0].astype(BF16)), run, ustrict, None)
            acc = acc + _dot_nt(a.astype(BF16), buf[slot, g, 1].astype(BF16))
        return c + 1, (jnp.max(run) >= -EXP_ZERO).astype(jnp.int32), run, acc

    start(0, 0)
    c_end, _, _, acc = lax.while_loop(
        lambda st: (st[0] < n_chunks) & (st[1] > 0), body,
        (jnp.int32(0), jnp.int32(1), jnp.zeros((SB_H, 1), F32), jnp.zeros((SB_H, SB_W), F32)))

    @pl.when(c_end < n_chunks)
    def _():
        wait(c_end, lax.rem(c_end, 2))

    o_ref[0] = _head_diag(acc, HEAD_DIM)


def _sb_decode(page_table, qrows, cache_t, li, n_pages):
    bd, npg = page_table.shape
    assert npg % n_pages == 0
    grid_spec = pltpu.PrefetchScalarGridSpec(
        num_scalar_prefetch=1,
        grid=(bd,),
        in_specs=[pl.BlockSpec((1, SB_H, SB_W), lambda b, pt: (b, 0, 0)), pl.BlockSpec(memory_space=pl.ANY)],
        out_specs=pl.BlockSpec((1, 1, SB_W), lambda b, pt: (b, 0, 0)),
        scratch_shapes=[pltpu.VMEM((2, n_pages, 2, SB_W, PAGE_SIZE), F32), pltpu.SemaphoreType.DMA((2,))])
    return pl.pallas_call(
        functools.partial(_sb_decode_kernel, li=li, n_pages=n_pages),
        grid_spec=grid_spec,
        out_shape=jax.ShapeDtypeStruct((bd, 1, SB_W), F32),
        compiler_params=_cparams("arbitrary"),
        name="sb_decode",
    )(page_table, qrows, cache_t)


def _nsa_decode_kernel(pt_ref, q_ref, *refs, n_pages, n_top):
    page_refs = refs[:n_pages]
    (win_ref, new_ref, gate_ref, tab_ref, expand_ref, pe_ref, w1_ref, w2_ref, o_ref,
     kc_sc, vc_sc, ks_sc, vs_sc) = refs[n_pages:]
    p = pl.program_id(1)
    past = kc_sc.shape[0]
    nb = past // CMP_BLOCK
    for g, page_ref in enumerate(page_refs):
        rows = pl.ds(pl.multiple_of((p * n_pages + g) * PAGE_SIZE, PAGE_SIZE), PAGE_SIZE)
        kc_sc[rows, :] = page_ref[0, 0, 0].T
        vc_sc[rows, :] = page_ref[0, 0, 1].T
        ks_sc[:, rows] = page_ref[0, 0, 2].astype(BF16)
        vs_sc[:, rows] = page_ref[0, 0, 3].astype(BF16)

    @pl.when(p == pl.num_programs(1) - 1)
    def _():
        q = q_ref[0]
        qf = q.astype(F32)
        tab = lambda b: tab_ref[:, b:b + 1]
        new = new_ref[0].astype(BF16).astype(F32)

        def with_new(s, mask, k_new, v_new, v_past_t):
            s_n = jnp.sum(qf * k_new, axis=1, keepdims=True) + tab(0)
            s = jnp.where(mask, s, NEG_INF)
            m = jnp.maximum(jnp.max(s, axis=1, keepdims=True), s_n)
            e = jnp.where(mask, jnp.exp(s - m), 0.0)
            e_n = jnp.exp(s_n - m)
            l = jnp.sum(e, axis=1, keepdims=True) + e_n
            acc = _dot_nt(e.astype(BF16), v_past_t) + e_n.astype(BF16).astype(F32) * v_new
            return acc / jnp.maximum(l, 1e-30)

        kc, vc = _compress_rows(kc_sc, vc_sc, nb, pe_ref, w1_ref, w2_ref)
        colc = lax.broadcasted_iota(jnp.int32, (NSA_H, nb), 1)
        dist_c = past - (colc * CMP_BLOCK + (CMP_BLOCK - 1))
        s_c = _dot_nt(q, kc.astype(BF16)) + _bias_from_dist(dist_c, tab)
        p_c = _masked_softmax(s_c, dist_c >= 0)
        o_c = _dot(p_c.astype(BF16), vc.astype(BF16))

        rg = lax.broadcasted_iota(jnp.int32, (NSA_H, NSA_H), 0) // NSA_GROUP
        cg = lax.broadcasted_iota(jnp.int32, (NSA_H, NSA_H), 1) // NSA_GROUP
        same_group = jnp.where(rg == cg, 1.0, 0.0).astype(BF16)
        score = _dot_exact01(p_c, same_group, 3, left=True)
        cur = past // CMP_BLOCK
        forced = (colc == 0) | (colc == cur) | (colc == cur - 1)
        score = jnp.where(forced, NSA_GROUP + 1.0, jnp.where(colc <= cur, score, -1.0))
        msel = _top_blocks(score, n_top - 1).astype(BF16)

        cols = lax.broadcasted_iota(jnp.int32, (NSA_H, past), 1)
        dist_s = past - cols
        s_s = _dot(q, ks_sc[...]) + _bias_from_dist(dist_s, tab)
        o_s = with_new(s_s, _dot(msel, expand_ref[...]) > 0.5, new[0:1], new[1:2], vs_sc[...])

        wb = win_ref.shape[-1]
        colw = lax.broadcasted_iota(jnp.int32, (NSA_H, wb), 1)
        dist_w = wb - colw
        s_w = _dot(q, win_ref[0, 0, 0].astype(BF16)) + _bias_from_dist(dist_w, tab)
        o_w = with_new(s_w, dist_w <= WINDOW, new[2:3], new[3:4], win_ref[0, 0, 1].astype(BF16))

        gt = _sigmoid(gate_ref[0])
        o = gt[:, 0:1] * o_c + gt[:, 1:2] * o_s + gt[:, 2:3] * o_w
        first_group = lax.broadcasted_iota(jnp.int32, o.shape, 0) < NSA_GROUP
        o_ref[0] = jnp.where(first_group, o, pltpu.roll(o, HEAD_DIM, 1))


def _nsa_decode(page_table, qg, cache_t, win_t, new_rows, gates, tab_t, expand, pe, w1bd, w2bd, li, n_pages):
    bd, npg = page_table.shape
    past = npg * PAGE_SIZE
    nb = past // CMP_BLOCK
    wb = win_t.shape[-1]
    assert npg % n_pages == 0
    full = lambda a: pl.BlockSpec(a.shape, lambda b, p, pt: (0,) * a.ndim)
    per_seq = lambda a: pl.BlockSpec((1,) + a.shape[1:], lambda b, p, pt: (b,) + (0,) * (a.ndim - 1))
    grid_spec = pltpu.PrefetchScalarGridSpec(
        num_scalar_prefetch=1,
        grid=(bd, npg // n_pages),
        in_specs=[per_seq(qg)]
        + _page_specs(n_pages, (1, 1, 4, LANES, PAGE_SIZE), lambda p, g: p * n_pages + g, li)
        + [pl.BlockSpec((1, 1, 2, LANES, wb), lambda b, p, pt: (b, li, 0, 0, 0)),
           per_seq(new_rows), per_seq(gates), full(tab_t), full(expand), full(pe), full(w1bd), full(w2bd)],
        out_specs=pl.BlockSpec((1, NSA_H, LANES), lambda b, p, pt: (b, 0, 0)),
        scratch_shapes=[pltpu.VMEM((past, LANES), F32), pltpu.VMEM((past, LANES), F32),
                        pltpu.VMEM((LANES, past), BF16), pltpu.VMEM((LANES, past), BF16)])
    return pl.pallas_call(
        functools.partial(_nsa_decode_kernel, n_pages=n_pages, n_top=min(TOP_N, nb + 1)),
        grid_spec=grid_spec,
        out_shape=jax.ShapeDtypeStruct((bd, NSA_H, LANES), F32),
        compiler_params=_cparams("parallel", "arbitrary"),
        name="nsa_decode",
    )(page_table, qg, *([cache_t] * n_pages), win_t, new_rows, gates, tab_t, expand, pe, w1bd, w2bd)


def _prep_even_weights(w_in, b_f, cmp_pos, w1, w2):
    w = jnp.concatenate([w_in[:, 0:1280], w_in[:, 1304:1816], w_in[:, 1816:3352], w_in[:, 3360:3872],
                         w_in[:, 1280:1304], w_in[:, 3352:3360],
                         jnp.zeros((D_MODEL, LANES - SMALL_GATE - FOX_H), w_in.dtype)], axis=1).astype(BF16)
    bfp = jnp.zeros((1, LANES), F32).at[0, SMALL_GATE:SMALL_GATE + FOX_H].set(b_f)
    w1r = w1.reshape(2, CMP_BLOCK, HEAD_DIM, HEAD_DIM)
    z1 = jnp.zeros_like(w1r)
    w1bd = jnp.concatenate([jnp.concatenate([w1r, z1], -1), jnp.concatenate([z1, w1r], -1)], -2).astype(BF16)
    pe = jnp.concatenate([cmp_pos, cmp_pos], -1)
    z2 = jnp.zeros_like(w2)
    w2dup = jnp.concatenate([jnp.concatenate([w2, w2, z2, z2], -1), jnp.concatenate([z2, z2, w2, w2], -1)], -2).astype(BF16)
    w2bd = jnp.concatenate([jnp.concatenate([w2, z2], -1), jnp.concatenate([z2, w2], -1)], -2).astype(BF16)
    return w, bfp, pe, w1bd, w2dup, w2bd


def _head_rows(q, n_heads):
    bd = q.shape[0]
    q3 = q.reshape(bd, 1, n_heads, HEAD_DIM)
    eye = jnp.eye(n_heads, dtype=q.dtype).reshape(1, n_heads, n_heads, 1)
    return (q3 * eye).reshape(bd, n_heads, n_heads * HEAD_DIM)


def _group_rows(q):
    bd = q.shape[0]
    q3 = q.reshape(bd, NSA_H, HEAD_DIM)
    z = jnp.zeros_like(q3)
    first = (jnp.arange(NSA_H) < NSA_GROUP).reshape(1, NSA_H, 1)
    return jnp.where(first, jnp.concatenate([q3, z], -1), jnp.concatenate([z, q3], -1))


def kernel(x_prompt, x_sample, cache_nsa, cache_nsa_win, cache_fox, cache_fox_logf, cache_sb, page_table, norm_g, final_g, rel_bias, w_in_even, w_out_even, b_forget, cmp_pos, w_cmp1, w_cmp2, w_in_odd, w_out_odd):
    b, s, d = x_prompt.shape
    bd, ds_, _ = x_sample.shape
    assert ds_ == 1 and d == D_MODEL
    depth = norm_g.shape[0]
    n_even = w_in_even.shape[0]
    n_odd = w_in_odd.shape[0]
    n_phys = cache_nsa.shape[0]
    npg = page_table.shape[1]
    past = npg * PAGE_SIZE
    wb = cache_nsa_win.shape[2]
    t = min(ATT_T, s)
    tm_p = 256
    tm_o = 512

    xp = x_prompt.reshape(b * s, d)
    xs = x_sample.reshape(bd, d)
    gfin = final_g.reshape(1, d)
    rows_last = lambda a: jnp.transpose(a, (0, 1, 3, 4, 5, 2))
    cache_nsa_t = rows_last(cache_nsa).reshape(n_phys, n_even, 4, LANES, PAGE_SIZE)
    win_t = rows_last(cache_nsa_win).reshape(bd, n_even, 2, LANES, wb)
    cache_fox_t = rows_last(cache_fox).reshape(n_phys, n_even, 2, FOX_W, PAGE_SIZE)
    lft = jnp.transpose(cache_fox_logf.astype(F32), (0, 1, 3, 2))
    cache_sb_t = rows_last(cache_sb).reshape(n_phys, n_odd, 2, SB_W, PAGE_SIZE)
    n_pages = math.gcd(npg, DECODE_PAGES)
    tab_t = rel_bias.T
    expand = (jnp.arange(past)[None, :] // CMP_BLOCK == jnp.arange(past // CMP_BLOCK)[:, None]).astype(BF16)

    outs = {k: [] for k in ("nsa_s", "win_p", "win_s", "fox_s", "lf_p", "lf_s", "sb_s")}
    rows_t = fkv_t = kv_t = None
    for layer in range(depth):
        g = norm_g[layer].reshape(1, d)
        li = layer // 2
        final = layer == depth - 1
        if layer % 2 == 0:
            w, bfp, pe, w1bd, w2dup, w2bd = _prep_even_weights(w_in_even[li], b_forget[li], cmp_pos[li], w_cmp1[li],
                                                               w_cmp2[li])
            w_out = w_out_even[li].astype(BF16)
            stack = (li, n_even, b, w[:, E_ROWS:E_WIN].T, w[:, E_FK:E_FZ].T, (rows_t, fkv_t) if li > 0 else None)
            q, rows, selkv, win, winkv, small, nz, fq, fkv_t, fk, fv, fz, rows_t = _even_in(xp, g, w, bfp, tm_p, stack)
            r3 = lambda a: a.reshape(b, s, a.shape[-1])
            cmp = _compress_prompt(r3(rows), pe, w1bd, w2dup)
            o_n = _nsa_prompt(rel_bias, r3(q), r3(selkv), r3(winkv), cmp, r3(small), t)
            logf = r3(small)[:, :, SMALL_GATE:SMALL_GATE + FOX_H]
            c = _cumsum_lanes(jnp.transpose(logf, (0, 2, 1)))
            o_f = _fox_prompt(r3(fq), r3(fk), r3(fv), c.reshape(b, FOX_H // 2, 2, s), t)
            xp = _out_proj(xp, [(o_n.reshape(b * s, NSA_W), nz), (o_f.reshape(b * s, FOX_W), fz)], w_out, gfin, final,
                           tm_o)
            outs["win_p"].append(r3(win)[:, s - min(WINDOW, s):].reshape(b, min(WINDOW, s), 2, NSA_KVH, HEAD_DIM))
            outs["lf_p"].append(logf)
            q, rows, selkv, win, winkv, small, nz, fq, fkv, fk, fv, fz = _even_in(xs, g, w, bfp, bd)
            new_rows = jnp.stack([rows[:, 2 * LANES:3 * LANES], rows[:, 3 * LANES:], win[:, :LANES], win[:, LANES:]],
                                 axis=1)
            gates = small[:, :SMALL_GATE].reshape(bd, NSA_H, 3)
            o8 = _nsa_decode(page_table, _group_rows(q), cache_nsa_t, win_t, new_rows, gates, tab_t, expand, pe, w1bd,
                             w2bd, li, n_pages)
            o_n = o8[:, :, :HEAD_DIM].reshape(bd, NSA_W)
            logf = small[:, SMALL_GATE:SMALL_GATE + FOX_H]
            o_f = _fox_decode(page_table, _head_rows(fq, FOX_H), cache_fox_t, lft, fkv.reshape(bd, 1, 2 * FOX_W),
                              logf.reshape(bd, FOX_H, 1), li, n_pages).reshape(bd, FOX_W)
            xs = _out_proj(xs, [(o_n, nz), (o_f, fz)], w_out, gfin, final, bd)
            outs["nsa_s"].append(rows.reshape(bd, 1, 4, NSA_KVH, HEAD_DIM))
            outs["win_s"].append(jnp.concatenate([cache_nsa_win[:, li, 1:], win.reshape(bd, 1, 2, NSA_KVH, HEAD_DIM)],
                                                 axis=1))
            outs["fox_s"].append(fkv.reshape(bd, 1, 2, FOX_H, HEAD_DIM))
            outs["lf_s"].append(logf.reshape(bd, 1, FOX_H))
        else:
            w = w_in_odd[li].astype(BF16)
            w_out = w_out_odd[li].astype(BF16)
            q, kv_t, k, v, z = _odd_in(xp, g, w, tm_p, (li, n_odd, b, w[:, SB_W:3 * SB_W].T, kv_t if li > 0 else None))
            r3 = lambda a: a.reshape(b, s, a.shape[-1])
            o = _sb_prompt(r3(q), r3(k), r3(v), min(SB_T, s))
            xp = _out_proj(xp, [(o.reshape(b * s, SB_W), z)], w_out, gfin, final, tm_o)
            q, kv, k, v, z = _odd_in(xs, g, w, bd)
            o = _sb_decode(page_table, _head_rows(q, SB_H), cache_sb_t, li,
                           math.gcd(npg, SB_DECODE_PAGES)).reshape(bd, SB_W)
            xs = _out_proj(xs, [(o, z)], w_out, gfin, final, bd)
            outs["sb_s"].append(kv.reshape(bd, 1, 2, SB_H, HEAD_DIM))

    st = lambda k: jnp.stack(outs[k], 1)
    rows_of = lambda a, kinds, heads: jnp.transpose(a.reshape(b, a.shape[1], kinds, heads, HEAD_DIM, s), (0, 1, 5, 2, 3, 4))
    return (xp.reshape(b, s, d), xs.reshape(bd, 1, d), rows_of(rows_t, 4, NSA_KVH), st("nsa_s"), st("win_p"), st("win_s"),
            rows_of(fkv_t, 2, FOX_H), st("fox_s"), st("lf_p"), st("lf_s"), rows_of(kv_t, 2, SB_H), st("sb_s"))
```

```python
import functools
import math

import numpy as np
import jax
import jax.numpy as jnp
from jax import lax
from jax.experimental import pallas as pl
from jax.experimental.pallas import tpu as pltpu

F32 = jnp.float32
BF16 = jnp.bfloat16

D_MODEL = 1024
HEAD_DIM = 64
LANES = 128
NSA_H = 8
NSA_KVH = 2
NSA_GROUP = NSA_H // NSA_KVH
FOX_H = 8
SB_H = 16
NSA_W = NSA_H * HEAD_DIM
FOX_W = FOX_H * HEAD_DIM
SB_W = SB_H * HEAD_DIM
CMP_BLOCK = 64
TOP_N = 16
WINDOW = 512
PAGE_SIZE = 128
N_BUCKETS = 32
REL_MAX_DIST = 128
RMS_EPS = 1e-6
NEG_INF = -1e30
SCALE = HEAD_DIM ** -0.5
EXP_ZERO = 104.0

ATT_T = 512
SB_T = 256
DECODE_PAGES = 16
SB_DECODE_PAGES = 2
VMEM_LIMIT = 56 * 1024 * 1024


def _cparams(*sem):
    return pltpu.CompilerParams(dimension_semantics=sem, vmem_limit_bytes=VMEM_LIMIT)


def _dot(a, b):
    return jnp.dot(a, b, preferred_element_type=F32)


def _dot_nt(a, b):
    return lax.dot_general(a, b, (((1,), (1,)), ((), ())), preferred_element_type=F32)


def _split2(x):
    hi = x.astype(BF16)
    lo = (x - hi.astype(F32)).astype(BF16)
    return hi, lo


def _dot_exact01(x, w01, passes, left=False):
    acc = None
    r = x
    for _ in range(passes):
        piece = r.astype(BF16)
        term = _dot(w01, piece) if left else _dot(piece, w01)
        acc = term if acc is None else acc + term
        r = r - piece.astype(F32)
    return acc


def _softplus(z):
    return jnp.maximum(z, 0.0) + jnp.log1p(jnp.exp(-jnp.abs(z)))


def _sigmoid(z):
    return 1.0 / (1.0 + jnp.exp(-z))


def _rms(x, g):
    return x * lax.rsqrt(jnp.mean(x * x, axis=-1, keepdims=True) + RMS_EPS) * g


def _lane_lo(shape):
    return lax.broadcasted_iota(jnp.int32, shape, len(shape) - 1) < HEAD_DIM


def _dup_halves(x):
    r = pltpu.roll(x, HEAD_DIM, 1)
    lo = _lane_lo(x.shape)
    return jnp.where(lo, x, r), jnp.where(lo, r, x)


E_Q, E_ROWS, E_WIN, E_NZ, E_FQ, E_FK, E_FV, E_FZ, E_SMALL, E_END = (
    0, 512, 1024, 1280, 1792, 2304, 2816, 3328, 3840, 3968)
SMALL_GATE = 3 * NSA_H


def _even_in_kernel(*refs, stacked, aliased, layer):
    x_ref, g_ref, w_ref, bf_ref = refs[:4]
    n_in = 4 + (2 if stacked else 0) + (2 if aliased else 0)
    (q_ref, rows_ref, selkv_ref, win_ref, winkv_ref, small_ref, nz_ref, fq_ref, fkv_ref, fk_ref, fv_ref,
     fz_ref) = refs[n_in:n_in + 12]
    xn = _rms(x_ref[...], g_ref[...]).astype(BF16)

    def mm(a, b):
        return _dot(xn, w_ref[:, a:b])

    q_ref[...] = (mm(E_Q, E_ROWS) * SCALE).astype(BF16)
    rows = mm(E_ROWS, E_WIN)
    rows_ref[...] = rows
    kd0, kd1 = _dup_halves(rows[:, 256:384])
    vd0, vd1 = _dup_halves(rows[:, 384:512])
    selkv_ref[...] = jnp.concatenate([kd0, kd1, vd0, vd1], axis=1).astype(BF16)
    win = mm(E_WIN, E_NZ)
    win_ref[...] = win
    kd0, kd1 = _dup_halves(win[:, 0:128])
    vd0, vd1 = _dup_halves(win[:, 128:256])
    winkv_ref[...] = jnp.concatenate([kd0, kd1, vd0, vd1], axis=1).astype(BF16)
    nz_ref[...] = mm(E_NZ, E_FQ)
    fq_ref[...] = (mm(E_FQ, E_FK) * SCALE).astype(BF16)
    fkv = mm(E_FK, E_FZ)
    fk_ref[...] = fkv[:, :FOX_W].astype(BF16)
    fv_ref[...] = fkv[:, FOX_W:].astype(BF16)
    if stacked:
        rows_t_ref = refs[n_in + 12]
        _store_layer(rows_t_ref, _dot_nt(refs[4][...], xn), layer, aliased)
        _store_layer(fkv_ref, _dot_nt(refs[5][...], xn), layer, aliased)
    else:
        fkv_ref[...] = fkv
    fz_ref[...] = mm(E_FZ, E_SMALL)
    small = mm(E_SMALL, E_END)
    lane = lax.broadcasted_iota(jnp.int32, small.shape, 1)
    is_f = (lane >= SMALL_GATE) & (lane < SMALL_GATE + FOX_H)
    small_ref[...] = jnp.where(is_f, -_softplus(-(small + bf_ref[...])), small)


def _stacked_spec(cols, tm, tiles_per_batch, li, n_layers, aliased):
    if aliased:
        return pl.BlockSpec((1, 1, cols, tm), lambda i: (i // tiles_per_batch, li, 0, i % tiles_per_batch))
    return pl.BlockSpec((1, n_layers, cols, tm), lambda i: (i // tiles_per_batch, 0, 0, i % tiles_per_batch))


def _store_layer(ref, data, li, aliased):
    if aliased:
        ref[0, 0] = data
    else:
        for layer in range(ref.shape[1]):
            ref[0, layer] = data if layer == li else jnp.zeros_like(data)


def _even_in(x, g, w, bfp, tm, stack=None):
    m = x.shape[0]
    row = lambda c: pl.BlockSpec((tm, c), lambda i: (i, 0))
    full = lambda a: pl.BlockSpec(a.shape, lambda i: (0, 0))
    outs = [(512, BF16), (512, F32), (512, BF16), (256, F32), (512, BF16), (128, F32),
            (512, F32), (512, BF16), (1024, F32), (512, BF16), (512, BF16), (512, F32)]
    in_specs = [row(D_MODEL), full(g), full(w), full(bfp)]
    args = [x, g, w, bfp]
    out_specs = [row(c) for c, _ in outs]
    out_shape = [jax.ShapeDtypeStruct((m, c), dt) for c, dt in outs]
    aliases = {}
    if stack is not None:
        li, n_layers, b, wt_rows, wt_fkv, prev = stack
        s = m // b
        tpb = s // tm
        in_specs += [full(wt_rows), full(wt_fkv)]
        args += [wt_rows, wt_fkv]
        out_specs[8] = _stacked_spec(2 * FOX_W, tm, tpb, li, n_layers, prev is not None)
        out_shape[8] = jax.ShapeDtypeStruct((b, n_layers, 2 * FOX_W, s), F32)
        out_specs.append(_stacked_spec(4 * LANES, tm, tpb, li, n_layers, prev is not None))
        out_shape.append(jax.ShapeDtypeStruct((b, n_layers, 4 * LANES, s), F32))
        if prev is not None:
            in_specs += [pl.BlockSpec(memory_space=pl.ANY)] * 2
            args += [prev[0], prev[1]]
            aliases = {6: 12, 7: 8}
    return pl.pallas_call(
        functools.partial(_even_in_kernel, stacked=stack is not None, aliased=bool(aliases),
                          layer=None if stack is None else stack[0]),
        grid=(m // tm,),
        in_specs=in_specs,
        out_specs=out_specs,
        out_shape=out_shape,
        input_output_aliases=aliases,
        compiler_params=_cparams("parallel"),
        name="even_in",
    )(*args)


def _odd_in_kernel(*refs, stacked, aliased, layer):
    x_ref, g_ref, w_ref = refs[:3]
    n_in = 3 + (1 if stacked else 0) + (1 if aliased else 0)
    q_ref, kv_ref, k_ref, v_ref, z_ref = refs[n_in:]
    xn = _rms(x_ref[...], g_ref[...]).astype(BF16)
    q_ref[...] = (_dot(xn, w_ref[:, 0:SB_W]) * SCALE).astype(BF16)
    kv = _dot(xn, w_ref[:, SB_W:3 * SB_W])
    if stacked:
        _store_layer(kv_ref, _dot_nt(refs[3][...], xn), layer, aliased)
    else:
        kv_ref[...] = kv
    k_ref[...] = kv[:, :SB_W].astype(BF16)
    v_ref[...] = kv[:, SB_W:].astype(BF16)
    z_ref[...] = _dot(xn, w_ref[:, 3 * SB_W:])


def _odd_in(x, g, w, tm, stack=None):
    m = x.shape[0]
    row = lambda c: pl.BlockSpec((tm, c), lambda i: (i, 0))
    full = lambda a: pl.BlockSpec(a.shape, lambda i: (0, 0))
    outs = [(SB_W, BF16), (2 * SB_W, F32), (SB_W, BF16), (SB_W, BF16), (SB_W, F32)]
    in_specs = [row(D_MODEL), full(g), full(w)]
    args = [x, g, w]
    out_specs = [row(c) for c, _ in outs]
    out_shape = [jax.ShapeDtypeStruct((m, c), dt) for c, dt in outs]
    aliases = {}
    if stack is not None:
        li, n_layers, b, wt_kv, prev = stack
        s = m // b
        in_specs.append(full(wt_kv))
        args.append(wt_kv)
        out_specs[1] = _stacked_spec(2 * SB_W, tm, s // tm, li, n_layers, prev is not None)
        out_shape[1] = jax.ShapeDtypeStruct((b, n_layers, 2 * SB_W, s), F32)
        if prev is not None:
            in_specs.append(pl.BlockSpec(memory_space=pl.ANY))
            args.append(prev)
            aliases = {4: 1}
    return pl.pallas_call(
        functools.partial(_odd_in_kernel, stacked=stack is not None, aliased=bool(aliases),
                          layer=None if stack is None else stack[0]),
        grid=(m // tm,),
        in_specs=in_specs,
        out_specs=out_specs,
        out_shape=out_shape,
        input_output_aliases=aliases,
        compiler_params=_cparams("parallel"),
        name="odd_in",
    )(*args)


def _out_kernel(n_parts, final, *refs):
    x_ref = refs[0]
    parts = refs[1:1 + 2 * n_parts]
    w_ref = refs[1 + 2 * n_parts]
    gf_ref = refs[2 + 2 * n_parts]
    y_ref = refs[3 + 2 * n_parts]
    y = x_ref[...]
    off = 0
    for p in range(n_parts):
        o = parts[2 * p][...]
        z = parts[2 * p + 1][...]
        k = o.shape[1]
        y = y + _dot((o * (z * _sigmoid(z))).astype(BF16), w_ref[off:off + k, :])
        off += k
    if final:
        y = _rms(y, gf_ref[...])
    y_ref[...] = y


def _out_proj(x, parts, w, gf, final, tm):
    m = x.shape[0]
    row = lambda c: pl.BlockSpec((tm, c), lambda i: (i, 0))
    full = lambda a: pl.BlockSpec(a.shape, lambda i: (0, 0))
    flat = [a for pr in parts for a in pr]
    return pl.pallas_call(
        functools.partial(_out_kernel, len(parts), final),
        grid=(m // tm,),
        in_specs=[row(D_MODEL)] + [row(a.shape[1]) for a in flat] + [full(w), full(gf)],
        out_specs=row(D_MODEL),
        out_shape=jax.ShapeDtypeStruct((m, D_MODEL), F32),
        compiler_params=_cparams("parallel"),
        name="out_proj",
    )(x, *flat, w, gf)


def _cumsum_kernel(x_ref, o_ref, *, chunk):
    n = x_ref.shape[2] // chunk
    r = lax.broadcasted_iota(jnp.int32, (chunk, chunk), 0)
    c = lax.broadcasted_iota(jnp.int32, (chunk, chunk), 1)
    tri = jnp.where(r <= c, 1.0, 0.0).astype(BF16)

    def body(j, carry):
        x = x_ref[0, :, pl.ds(j * chunk, chunk)]
        cs = _dot_exact01(x, tri, 3) + carry
        o_ref[0, :, pl.ds(j * chunk, chunk)] = cs
        return cs[:, chunk - 1:chunk]

    lax.fori_loop(0, n, body, jnp.zeros((x_ref.shape[1], 1), F32))


def _cumsum_lanes(x):
    b, h, s = x.shape
    spec = pl.BlockSpec((1, h, s), lambda i: (i, 0, 0))
    return pl.pallas_call(
        functools.partial(_cumsum_kernel, chunk=min(256, s)),
        grid=(b,),
        in_specs=[spec],
        out_specs=spec,
        out_shape=jax.ShapeDtypeStruct(x.shape, F32),
        compiler_params=_cparams("parallel"),
        name="logf_cumsum",
    )(x)


def _fox_prompt_kernel(q_ref, k_ref, v_ref, c_ref, o_ref, kmax_sc, *, t):
    i = pl.program_id(2)
    q2 = q_ref[0]
    lo = _lane_lo(q2.shape)
    zero = jnp.zeros_like(q2)
    qs = (jnp.where(lo, q2, zero), jnp.where(lo, zero, q2))
    row = lax.broadcasted_iota(jnp.int32, (t, t), 0)
    col = lax.broadcasted_iota(jnp.int32, (t, t), 1)
    causal = row >= col

    @pl.when(i == 0)
    def _():
        def norm_step(cidx, mx):
            kk = k_ref[0, pl.ds(cidx * t, t), :].astype(F32)
            kk = kk * kk
            n0 = jnp.max(jnp.sum(jnp.where(lo, kk, 0.0), axis=1, keepdims=True), axis=0, keepdims=True)
            n1 = jnp.max(jnp.sum(jnp.where(lo, 0.0, kk), axis=1, keepdims=True), axis=0, keepdims=True)
            return jnp.maximum(mx[0], n0), jnp.maximum(mx[1], n1)

        z11 = jnp.zeros((1, 1), F32)
        mx = lax.fori_loop(0, k_ref.shape[1] // t, norm_step, (z11, z11))
        for h in range(2):
            kmax_sc[h] = jnp.broadcast_to(jnp.sqrt(mx[h]), kmax_sc.shape[1:])

    qf = q2.astype(F32)
    qq = qf * qf
    bound = []
    for h in range(2):
        qn = jnp.sqrt(jnp.sum(jnp.where(lo, qq, 0.0) if h == 0 else jnp.where(lo, 0.0, qq), axis=1, keepdims=True))
        bound.append(qn * kmax_sc[h][0:1, 0:1] + 1.0)

    def step(j, carry, masked):
        kblk = k_ref[0, pl.ds(j * t, t), :]
        vblk = v_ref[0, pl.ds(j * t, t), :]
        cb = c_ref[0, 0, :, pl.ds(j * t, t)]
        new = []
        for h in range(2):
            m_old, l_old, acc = carry[h]
            s = _dot_nt(qs[h], kblk) - cb[h:h + 1, :]
            if masked:
                s = jnp.where(causal, s, NEG_INF)
            m_new = jnp.maximum(m_old, jnp.max(s, axis=1, keepdims=True))
            alpha = jnp.exp(m_old - m_new)
            p = jnp.exp(s - m_new)
            if masked:
                p = jnp.where(causal, p, 0.0)
            l_new = alpha * l_old + jnp.sum(p, axis=1, keepdims=True)
            acc = alpha * acc + _dot(p.astype(BF16), vblk)
            new.append((m_new, l_new, acc))
        return tuple(new)

    def older_tiles_matter(jj, carry):
        cend = c_ref[0, 0, :, pl.ds(jnp.maximum(jj, 0) * t, t)][:, t - 1:t]
        gap = [jnp.max(bound[h] - carry[h][0], axis=0, keepdims=True) - cend[h:h + 1, :] for h in range(2)]
        return (jnp.max(jnp.maximum(gap[0], gap[1])) >= -EXP_ZERO).astype(jnp.int32)

    def body(state):
        jj, _, carry = state
        carry = step(jj, carry, False)
        return jj - 1, older_tiles_matter(jj - 1, carry), carry

    init = tuple((jnp.full((t, 1), NEG_INF, F32), jnp.zeros((t, 1), F32), jnp.zeros((t, LANES), F32))
                 for _ in range(2))
    carry = step(i, init, True)
    _, _, carry = lax.while_loop(lambda st: (st[0] >= 0) & (st[1] > 0), body,
                                 (i - 1, older_tiles_matter(i - 1, carry), carry))
    outs = [acc / jnp.maximum(l, 1e-30) for (_, l, acc) in carry]
    o_ref[0] = jnp.where(lo, outs[0], outs[1])


def _fox_prompt(q, k, v, c4, t):
    b, s, w = q.shape
    hp = w // LANES
    return pl.pallas_call(
        functools.partial(_fox_prompt_kernel, t=t),
        grid=(b, hp, s // t),
        in_specs=[pl.BlockSpec((1, t, LANES), lambda bi, h, i: (bi, i, h)),
                  pl.BlockSpec((1, s, LANES), lambda bi, h, i: (bi, 0, h)),
                  pl.BlockSpec((1, s, LANES), lambda bi, h, i: (bi, 0, h)),
                  pl.BlockSpec((1, 1, 2, s), lambda bi, h, i: (bi, h, 0, 0))],
        out_specs=pl.BlockSpec((1, t, LANES), lambda bi, h, i: (bi, i, h)),
        out_shape=jax.ShapeDtypeStruct((b, s, w), F32),
        scratch_shapes=[pltpu.VMEM((2, 8, LANES), F32)],
        compiler_params=_cparams("parallel", "parallel", "arbitrary"),
        name="fox_prompt",
    )(q, k, v, c4)


def _sb_weights(z, after_carry, ustrict, mask):
    sp = _softplus(z)
    l1m = -sp if mask is None else jnp.where(mask, -sp, 0.0)
    after = _dot_exact01(l1m, ustrict, 2) + after_carry
    a = jnp.exp(z - sp + after)
    if mask is not None:
        a = jnp.where(mask, a, 0.0)
    return a, after_carry + jnp.sum(l1m, axis=1, keepdims=True)


def _sb_prompt_kernel(q_ref, k_ref, v_ref, o_ref, *, t):
    i = pl.program_id(2)
    q2 = q_ref[0]
    lo = _lane_lo(q2.shape)
    zero = jnp.zeros_like(q2)
    qs = (jnp.where(lo, q2, zero), jnp.where(lo, zero, q2))
    row = lax.broadcasted_iota(jnp.int32, (t, t), 0)
    col = lax.broadcasted_iota(jnp.int32, (t, t), 1)
    strict = row > col
    ustrict = jnp.where(strict, 1.0, 0.0).astype(BF16)

    def step(j, carry, mask):
        kblk = k_ref[0, pl.ds(j * t, t), :]
        vblk = v_ref[0, pl.ds(j * t, t), :]
        new = []
        for h in range(2):
            run, acc = carry[h]
            a, run = _sb_weights(_dot_nt(qs[h], kblk), run, ustrict, mask)
            new.append((run, acc + _dot(a.astype(BF16), vblk)))
        return tuple(new)

    def older_tiles_matter(carry):
        return (jnp.max(jnp.maximum(carry[0][0], carry[1][0])) >= -EXP_ZERO).astype(jnp.int32)

    def body(state):
        jj, _, carry = state
        carry = step(jj, carry, None)
        return jj - 1, older_tiles_matter(carry), carry

    init = tuple((jnp.zeros((t, 1), F32), jnp.zeros((t, LANES), F32)) for _ in range(2))
    carry = step(i, init, strict)
    _, _, carry = lax.while_loop(lambda st: (st[0] >= 0) & (st[1] > 0), body,
                                 (i - 1, older_tiles_matter(carry), carry))
    o_ref[0] = jnp.where(lo, carry[0][1], carry[1][1])


def _sb_prompt(q, k, v, t):
    b, s, w = q.shape
    hp = w // LANES
    return pl.pallas_call(
        functools.partial(_sb_prompt_kernel, t=t),
        grid=(b, hp, s // t),
        in_specs=[pl.BlockSpec((1, t, LANES), lambda bi, h, i: (bi, i, h)),
                  pl.BlockSpec((1, s, LANES), lambda bi, h, i: (bi, 0, h)),
                  pl.BlockSpec((1, s, LANES), lambda bi, h, i: (bi, 0, h))],
        out_specs=pl.BlockSpec((1, t, LANES), lambda bi, h, i: (bi, i, h)),
        out_shape=jax.ShapeDtypeStruct((b, s, w), F32),
        compiler_params=_cparams("parallel", "parallel", "arbitrary"),
        name="sb_prompt",
    )(q, k, v)


def _bucket_starts():
    d = np.arange(0, 4 * REL_MAX_DIST)
    max_exact = N_BUCKETS // 2
    large = max_exact + (np.log(np.maximum(d, 1).astype(np.float32) / max_exact)
                         / math.log(REL_MAX_DIST / max_exact) * (N_BUCKETS - max_exact)).astype(np.int32)
    b = np.where(d < max_exact, d, np.minimum(large, N_BUCKETS - 1))
    return [int(np.min(np.nonzero(b >= k)[0])) for k in range(N_BUCKETS)]


BUCKET_STARTS = _bucket_starts()
BIAS_FAR_DIST = BUCKET_STARTS[-1]


def _bias_from_dist(dist, tab):
    v = jnp.where(dist >= BUCKET_STARTS[1], tab(1), tab(0))
    for b in range(2, N_BUCKETS):
        v = jnp.where(dist >= BUCKET_STARTS[b], tab(b), v)
    return v


def _masked_softmax(s, mask):
    logits = jnp.where(mask, s, NEG_INF)
    m = jnp.max(logits, axis=-1, keepdims=True)
    e = jnp.where(mask, jnp.exp(logits - m), 0.0)
    return e / jnp.maximum(jnp.sum(e, axis=-1, keepdims=True), 1e-30)


def _top_blocks(score, n_top):
    lanef = lax.broadcasted_iota(jnp.int32, score.shape, 1).astype(F32)
    work = score
    msel = jnp.zeros(score.shape, F32)
    for _ in range(n_top):
        mx = jnp.max(work, axis=1, keepdims=True)
        idx = jnp.min(jnp.where(work == mx, lanef, 1e9), axis=1, keepdims=True)
        pick = lanef == idx
        msel = jnp.where(pick, 1.0, msel)
        work = jnp.where(pick, -2.0, work)
    return msel


def _compress_rows(k_ref, v_ref, nb, pe_ref, w1_ref, w2_ref):
    outs = []
    for kv, ref in enumerate((k_ref, v_ref)):
        acc = jnp.zeros((nb, LANES), F32)
        for pos in range(CMP_BLOCK):
            x = ref[pl.ds(pos, nb, stride=CMP_BLOCK), :] + pe_ref[kv, pos:pos + 1, :]
            acc = acc + _dot(x.astype(BF16), w1_ref[kv, pos])
        hid = acc * _sigmoid(acc)
        outs.append(_dot(hid.astype(BF16), w2_ref[kv]))
    return outs


def _compress_kernel(k_ref, v_ref, pe_ref, w1_ref, w2_ref, o_ref):
    nb = o_ref.shape[1]
    kc, vc = _compress_rows(k_ref.at[0], v_ref.at[0], nb, pe_ref, w1_ref, w2_ref)
    o_ref[0] = jnp.concatenate([kc, vc], axis=1).astype(BF16)


def _compress_prompt(rows, pe, w1bd, w2big):
    b, s, _ = rows.shape
    nb = s // CMP_BLOCK
    full = lambda a: pl.BlockSpec(a.shape, lambda i: (0,) * a.ndim)
    return pl.pallas_call(
        _compress_kernel,
        grid=(b,),
        in_specs=[pl.BlockSpec((1, s, LANES), lambda i: (i, 0, 0)), pl.BlockSpec((1, s, LANES), lambda i: (i, 0, 1)),
                  full(pe), full(w1bd), full(w2big)],
        out_specs=pl.BlockSpec((1, nb, 4 * LANES), lambda i: (i, 0, 0)),
        out_shape=jax.ShapeDtypeStruct((b, nb, 4 * LANES), BF16),
        compiler_params=_cparams("parallel"),
        name="nsa_compress_prompt",
    )(rows, rows, pe, w1bd, w2big)


def _nsa_prompt_kernel(tab_ref, q_ref, ks_ref, vs_ref, kw_ref, vw_ref, kc_ref, vc_ref, gate_ref, o_ref, tiles_sc,
                       cmpb_sc, *, t, n_top):
    g = pl.program_id(1)
    i = pl.program_id(2)
    nh = NSA_GROUP
    nb = kc_ref.shape[1]
    bpt = t // CMP_BLOCK
    row = lax.broadcasted_iota(jnp.int32, (t, t), 0)
    col = lax.broadcasted_iota(jnp.int32, (t, t), 1)
    rowc = lax.broadcasted_iota(jnp.int32, (t, nb), 0)
    colc = lax.broadcasted_iota(jnp.int32, (t, nb), 1)

    @pl.when(i == 0)
    def _():
        for hh in range(nh):
            h = g * nh + hh
            tab = lambda b: tab_ref[b, h]
            far = tab_ref[N_BUCKETS - 1, h]
            for kind in range(2):
                tiles_sc[kind, hh] = _bias_from_dist(kind * t + row - col, tab) - far
            rel = jnp.where(colc < nb // 2, colc, colc - nb)
            dist0 = rowc - (rel * CMP_BLOCK + (CMP_BLOCK - 1))
            cmpb_sc[hh] = jnp.where(dist0 >= 0, _bias_from_dist(dist0, tab) - far, 0.0)

    q4 = q_ref[0]
    lo = _lane_lo((t, LANES))
    zero = jnp.zeros((t, LANES), BF16)
    qa, qb = q4[:, :LANES], q4[:, LANES:]
    qst = jnp.concatenate([jnp.where(lo, qa, zero), jnp.where(lo, zero, qa),
                           jnp.where(lo, qb, zero), jnp.where(lo, zero, qb)], axis=0)

    dist_c = (i * t + rowc) - (colc * CMP_BLOCK + (CMP_BLOCK - 1))
    bias_c = jnp.stack([pltpu.roll(cmpb_sc[hh], bpt * i, 1) for hh in range(nh)], axis=0)
    valid_c = (dist_c >= 0)[None]
    p_c = _masked_softmax(_dot_nt(qst, kc_ref[0]).reshape(nh, t, nb) + bias_c, valid_c)
    o_c = _dot(p_c.reshape(nh * t, nb).astype(BF16), vc_ref[0])

    cur = (i * t + rowc) // CMP_BLOCK
    forced = (colc == 0) | (colc == cur) | (colc == cur - 1)
    score = jnp.where(forced, NSA_GROUP + 1.0, jnp.where(colc <= cur, jnp.sum(p_c, axis=0), -1.0))
    msel = _top_blocks(score, n_top).astype(BF16)

    blk_of_key = lax.broadcasted_iota(jnp.int32, (nb, t), 0)
    key_in_tile = lax.broadcasted_iota(jnp.int32, (nb, t), 1)

    def flash_step(carry, j, k_ref, v_ref, mask, bias):
        m_old, l_old, acc = carry
        kblk = k_ref[0, pl.ds(j * t, t), :]
        vblk = v_ref[0, pl.ds(j * t, t), :]
        s = _dot_nt(qst, kblk).reshape(nh, t, t)
        if bias is not None:
            s = s + bias
        s = jnp.where(mask[None], s, NEG_INF)
        m_new = jnp.maximum(m_old, jnp.max(s, axis=-1, keepdims=True))
        alpha = jnp.exp(m_old - m_new)
        p = jnp.exp(s - m_new)
        l_new = alpha * l_old + jnp.sum(p, axis=-1, keepdims=True)
        acc = alpha.reshape(nh * t, 1) * acc + _dot(p.reshape(nh * t, t).astype(BF16), vblk)
        return m_new, l_new, acc

    def finish(carry):
        _, l, acc = carry
        return acc / jnp.maximum(l, 1e-30).reshape(nh * t, 1)

    def selected(j, on):
        expand = jnp.where(blk_of_key == jnp.right_shift(j * t + key_in_tile, 6), 1.0, 0.0).astype(BF16)
        return _dot(msel, expand) > jnp.where(on, 0.5, 2.0)

    dist0 = row - col
    prev = jnp.maximum(i - 1, 0)
    init = (jnp.full((nh, t, 1), NEG_INF, F32), jnp.zeros((nh, t, 1), F32), jnp.zeros((nh * t, LANES), F32))
    carry = flash_step(init, i, ks_ref, vs_ref, selected(i, True) & (dist0 >= 0), tiles_sc[0])
    carry = flash_step(carry, prev, ks_ref, vs_ref, selected(prev, i >= 1), tiles_sc[1])
    o_s = finish(lax.fori_loop(0, prev, lambda j, c: flash_step(c, j, ks_ref, vs_ref, selected(j, True), None), carry))

    carry = flash_step(init, i, kw_ref, vw_ref, (dist0 >= 0) & (dist0 <= WINDOW), tiles_sc[0])
    for k in range(1, (WINDOW + t - 1) // t + 1):
        reach = jnp.where(i >= k, WINDOW - k * t, -t - 1)
        carry = flash_step(carry, jnp.maximum(i - k, 0), kw_ref, vw_ref, dist0 <= reach,
                           tiles_sc[1] if k == 1 else None)
    o_w = finish(carry)

    sig = _sigmoid(gate_ref[0])
    lane = lax.broadcasted_iota(jnp.int32, sig.shape, 1)

    def gate(hh, br):
        return jnp.sum(jnp.where(lane == (g * nh + hh) * 3 + br, sig, 0.0), axis=1, keepdims=True)

    outs = []
    for hh in range(nh):
        sl = slice(hh * t, (hh + 1) * t)
        outs.append(gate(hh, 0) * o_c[sl] + gate(hh, 1) * o_s[sl] + gate(hh, 2) * o_w[sl])
    o_ref[0] = jnp.concatenate([jnp.where(lo, outs[0], outs[1]), jnp.where(lo, outs[2], outs[3])], axis=1)


def _nsa_prompt(tab, q, selkv, winkv, cmp, small, t):
    b, s, _ = q.shape
    nb = cmp.shape[1]
    assert t >= BIAS_FAR_DIST and s % t == 0
    kv = lambda off: pl.BlockSpec((1, s, LANES), lambda bi, g, i: (bi, 0, off + g))
    cm = lambda off: pl.BlockSpec((1, nb, LANES), lambda bi, g, i: (bi, 0, off + g))
    return pl.pallas_call(
        functools.partial(_nsa_prompt_kernel, t=t, n_top=min(TOP_N, nb)),
        grid=(b, NSA_KVH, s // t),
        in_specs=[pl.BlockSpec(memory_space=pltpu.SMEM),
                  pl.BlockSpec((1, t, 2 * LANES), lambda bi, g, i: (bi, i, g)),
                  kv(0), kv(2), kv(0), kv(2), cm(0), cm(2),
                  pl.BlockSpec((1, t, LANES), lambda bi, g, i: (bi, i, 0))],
        out_specs=pl.BlockSpec((1, t, 2 * LANES), lambda bi, g, i: (bi, i, g)),
        out_shape=jax.ShapeDtypeStruct((b, s, NSA_W), F32),
        scratch_shapes=[pltpu.VMEM((2, NSA_GROUP, t, t), F32), pltpu.VMEM((NSA_GROUP, t, nb), F32)],
        compiler_params=_cparams("parallel", "parallel", "arbitrary"),
        name="nsa_prompt",
    )(tab, q, selkv, selkv, winkv, winkv, cmp, cmp, small)


def _head_diag(x, width):
    r = lax.broadcasted_iota(jnp.int32, x.shape, 0)
    c = lax.broadcasted_iota(jnp.int32, x.shape, 1)
    own = (c >= r * width) & (c < (r + 1) * width)
    return jnp.sum(jnp.where(own, x, 0.0), axis=0, keepdims=True)


def _page_specs(n, block, page_of, li):
    return [pl.BlockSpec(block, functools.partial(
        lambda b, p, pt, g: (pt[b, page_of(p, g)], li) + (0,) * (len(block) - 2), g=g)) for g in range(n)]


def _fox_decode_kernel(pt_ref, q_ref, *refs, n_pages):
    page_refs = refs[:n_pages]
    lf_refs = refs[n_pages:2 * n_pages]
    kvn_ref, lfn_ref, o_ref, m_sc, l_sc, acc_sc, c_sc = refs[2 * n_pages:]
    p = pl.program_id(1)
    w = FOX_W

    @pl.when(p == 0)
    def _():
        m_sc[...] = jnp.full(m_sc.shape, NEG_INF, F32)
        l_sc[...] = jnp.zeros(l_sc.shape, F32)
        acc_sc[...] = jnp.zeros(acc_sc.shape, F32)
        c_sc[...] = jnp.zeros(c_sc.shape, F32)

    q = q_ref[0]
    r = lax.broadcasted_iota(jnp.int32, (PAGE_SIZE, PAGE_SIZE), 0)
    cidx = lax.broadcasted_iota(jnp.int32, (PAGE_SIZE, PAGE_SIZE), 1)
    tri = jnp.where(r <= cidx, 1.0, 0.0).astype(BF16)
    m, l, acc, c_run = m_sc[...], l_sc[...], acc_sc[...], c_sc[...]
    scores = []
    for page_ref, lf_ref in zip(page_refs, lf_refs):
        lf = lf_ref[0, 0]
        c = _dot_exact01(lf, tri, 3) + c_run
        c_run = c_run + jnp.sum(lf, axis=1, keepdims=True)
        scores.append(_dot(q, page_ref[0, 0, 0].astype(BF16)) - c)
    s = jnp.concatenate(scores, axis=1)
    m_new = jnp.maximum(m, jnp.max(s, axis=1, keepdims=True))
    alpha = jnp.exp(m - m_new)
    e = jnp.exp(s - m_new)
    l = alpha * l + jnp.sum(e, axis=1, keepdims=True)
    acc = alpha * acc
    for g, page_ref in enumerate(page_refs):
        acc = acc + _dot_nt(e[:, g * PAGE_SIZE:(g + 1) * PAGE_SIZE].astype(BF16), page_ref[0, 0, 1].astype(BF16))
    m = m_new
    m_sc[...], l_sc[...], acc_sc[...], c_sc[...] = m, l, acc, c_run

    @pl.when(p == pl.num_programs(1) - 1)
    def _():
        kvn = kvn_ref[0].astype(BF16).astype(F32)
        s_n = jnp.sum(q.astype(F32) * kvn[:, :w], axis=1, keepdims=True) - (c_run + lfn_ref[0])
        m_fin = jnp.maximum(m, s_n)
        a = jnp.exp(m - m_fin)
        e_n = jnp.exp(s_n - m_fin)
        l_fin = a * l + e_n
        out = a * acc + e_n.astype(BF16).astype(F32) * kvn[:, w:]
        o_ref[0] = _head_diag(out / jnp.maximum(l_fin, 1e-30), HEAD_DIM)


def _fox_decode(page_table, qrows, cache_t, lft, kv_new, lf_new, li, n_pages):
    bd, npg = page_table.shape
    h = FOX_H
    assert npg % n_pages == 0
    page_of = lambda p, g: p * n_pages + g
    grid_spec = pltpu.PrefetchScalarGridSpec(
        num_scalar_prefetch=1,
        grid=(bd, npg // n_pages),
        in_specs=[pl.BlockSpec((1, h, FOX_W), lambda b, p, pt: (b, 0, 0))]
        + _page_specs(n_pages, (1, 1, 2, FOX_W, PAGE_SIZE), page_of, li)
        + _page_specs(n_pages, (1, 1, h, PAGE_SIZE), page_of, li)
        + [pl.BlockSpec((1, 1, 2 * FOX_W), lambda b, p, pt: (b, 0, 0)),
           pl.BlockSpec((1, h, 1), lambda b, p, pt: (b, 0, 0))],
        out_specs=pl.BlockSpec((1, 1, FOX_W), lambda b, p, pt: (b, 0, 0)),
        scratch_shapes=[pltpu.VMEM((h, 1), F32), pltpu.VMEM((h, 1), F32), pltpu.VMEM((h, FOX_W), F32),
                        pltpu.VMEM((h, 1), F32)])
    return pl.pallas_call(
        functools.partial(_fox_decode_kernel, n_pages=n_pages),
        grid_spec=grid_spec,
        out_shape=jax.ShapeDtypeStruct((bd, 1, FOX_W), F32),
        compiler_params=_cparams("parallel", "arbitrary"),
        name="fox_decode",
    )(page_table, qrows, *([cache_t] * n_pages), *([lft] * n_pages), kv_new, lf_new)


def _sb_decode_kernel(pt_ref, q_ref, cache_ref, o_ref, buf, sem, *, li, n_pages):
    b = pl.program_id(0)
    npg = pt_ref.shape[1]
    n_chunks = npg // n_pages

    def page_copy(c, g, slot):
        page = pt_ref[b, npg - 1 - (c * n_pages + g)]
        return pltpu.make_async_copy(cache_ref.at[page, li], buf.at[slot, g], sem.at[slot])

    def start(c, slot):
        for g in range(n_pages):
            page_copy(c, g, slot).start()

    def wait(c, slot):
        for g in range(n_pages):
            page_copy(c, g, slot).wait()

    r = lax.broadcasted_iota(jnp.int32, (PAGE_SIZE, PAGE_SIZE), 0)
    cidx = lax.broadcasted_iota(jnp.int32, (PAGE_SIZE, PAGE_SIZE), 1)
    ustrict = jnp.where(r > cidx, 1.0, 0.0).astype(BF16)
    q = q_ref[0]

    def body(state):
        c, _, run, acc = state
        slot = lax.rem(c, 2)

        @pl.when(c + 1 < n_chunks)
        def _():
            start(c + 1, 1 - slot)

        wait(c, slot)
        for g in range(n_pages):
            a, run = _sb_weights(_dot(q, buf[slot, g, 0].astype(BF16)), run, ustrict, None)
            acc = acc + _dot_nt(a.astype(BF16), buf[slot, g, 1].astype(BF16))
        return c + 1, (jnp.max(run) >= -EXP_ZERO).astype(jnp.int32), run, acc

    start(0, 0)
    c_end, _, _, acc = lax.while_loop(
        lambda st: (st[0] < n_chunks) & (st[1] > 0), body,
        (jnp.int32(0), jnp.int32(1), jnp.zeros((SB_H, 1), F32), jnp.zeros((SB_H, SB_W), F32)))

    @pl.when(c_end < n_chunks)
    def _():
        wait(c_end, lax.rem(c_end, 2))

    o_ref[0] = _head_diag(acc, HEAD_DIM)


def _sb_decode(page_table, qrows, cache_t, li, n_pages):
    bd, npg = page_table.shape
    assert npg % n_pages == 0
    grid_spec = pltpu.PrefetchScalarGridSpec(
        num_scalar_prefetch=1,
        grid=(bd,),
        in_specs=[pl.BlockSpec((1, SB_H, SB_W), lambda b, pt: (b, 0, 0)), pl.BlockSpec(memory_space=pl.ANY)],
        out_specs=pl.BlockSpec((1, 1, SB_W), lambda b, pt: (b, 0, 0)),
        scratch_shapes=[pltpu.VMEM((2, n_pages, 2, SB_W, PAGE_SIZE), F32), pltpu.SemaphoreType.DMA((2,))])
    return pl.pallas_call(
        functools.partial(_sb_decode_kernel, li=li, n_pages=n_pages),
        grid_spec=grid_spec,
        out_shape=jax.ShapeDtypeStruct((bd, 1, SB_W), F32),
        compiler_params=_cparams("arbitrary"),
        name="sb_decode",
    )(page_table, qrows, cache_t)


def _nsa_decode_kernel(pt_ref, q_ref, *refs, n_pages, n_top):
    page_refs = refs[:n_pages]
    (win_ref, new_ref, gate_ref, tab_ref, expand_ref, pe_ref, w1_ref, w2_ref, o_ref,
     kc_sc, vc_sc, ks_sc, vs_sc) = refs[n_pages:]
    p = pl.program_id(1)
    past = kc_sc.shape[0]
    nb = past // CMP_BLOCK
    for g, page_ref in enumerate(page_refs):
        rows = pl.ds(pl.multiple_of((p * n_pages + g) * PAGE_SIZE, PAGE_SIZE), PAGE_SIZE)
        kc_sc[rows, :] = page_ref[0, 0, 0].T
        vc_sc[rows, :] = page_ref[0, 0, 1].T
        ks_sc[:, rows] = page_ref[0, 0, 2].astype(BF16)
        vs_sc[:, rows] = page_ref[0, 0, 3].astype(BF16)

    @pl.when(p == pl.num_programs(1) - 1)
    def _():
        q = q_ref[0]
        qf = q.astype(F32)
        tab = lambda b: tab_ref[:, b:b + 1]
        new = new_ref[0].astype(BF16).astype(F32)

        def with_new(s, mask, k_new, v_new, v_past_t):
            s_n = jnp.sum(qf * k_new, axis=1, keepdims=True) + tab(0)
            s = jnp.where(mask, s, NEG_INF)
            m = jnp.maximum(jnp.max(s, axis=1, keepdims=True), s_n)
            e = jnp.where(mask, jnp.exp(s - m), 0.0)
            e_n = jnp.exp(s_n - m)
            l = jnp.sum(e, axis=1, keepdims=True) + e_n
            acc = _dot_nt(e.astype(BF16), v_past_t) + e_n.astype(BF16).astype(F32) * v_new
            return acc / jnp.maximum(l, 1e-30)

        kc, vc = _compress_rows(kc_sc, vc_sc, nb, pe_ref, w1_ref, w2_ref)
        colc = lax.broadcasted_iota(jnp.int32, (NSA_H, nb), 1)
        dist_c = past - (colc * CMP_BLOCK + (CMP_BLOCK - 1))
        s_c = _dot_nt(q, kc.astype(BF16)) + _bias_from_dist(dist_c, tab)
        p_c = _masked_softmax(s_c, dist_c >= 0)
        o_c = _dot(p_c.astype(BF16), vc.astype(BF16))

        rg = lax.broadcasted_iota(jnp.int32, (NSA_H, NSA_H), 0) // NSA_GROUP
        cg = lax.broadcasted_iota(jnp.int32, (NSA_H, NSA_H), 1) // NSA_GROUP
        same_group = jnp.where(rg == cg, 1.0, 0.0).astype(BF16)
        score = _dot_exact01(p_c, same_group, 3, left=True)
        cur = past // CMP_BLOCK
        forced = (colc == 0) | (colc == cur) | (colc == cur - 1)
        score = jnp.where(forced, NSA_GROUP + 1.0, jnp.where(colc <= cur, score, -1.0))
        msel = _top_blocks(score, n_top - 1).astype(BF16)

        cols = lax.broadcasted_iota(jnp.int32, (NSA_H, past), 1)
        dist_s = past - cols
        s_s = _dot(q, ks_sc[...]) + _bias_from_dist(dist_s, tab)
        o_s = with_new(s_s, _dot(msel, expand_ref[...]) > 0.5, new[0:1], new[1:2], vs_sc[...])

        wb = win_ref.shape[-1]
        colw = lax.broadcasted_iota(jnp.int32, (NSA_H, wb), 1)
        dist_w = wb - colw
        s_w = _dot(q, win_ref[0, 0, 0].astype(BF16)) + _bias_from_dist(dist_w, tab)
        o_w = with_new(s_w, dist_w <= WINDOW, new[2:3], new[3:4], win_ref[0, 0, 1].astype(BF16))

        gt = _sigmoid(gate_ref[0])
        o = gt[:, 0:1] * o_c + gt[:, 1:2] * o_s + gt[:, 2:3] * o_w
        first_group = lax.broadcasted_iota(jnp.int32, o.shape, 0) < NSA_GROUP
        o_ref[0] = jnp.where(first_group, o, pltpu.roll(o, HEAD_DIM, 1))


def _nsa_decode(page_table, qg, cache_t, win_t, new_rows, gates, tab_t, expand, pe, w1bd, w2bd, li, n_pages):
    bd, npg = page_table.shape
    past = npg * PAGE_SIZE
    nb = past // CMP_BLOCK
    wb = win_t.shape[-1]
    assert npg % n_pages == 0
    full = lambda a: pl.BlockSpec(a.shape, lambda b, p, pt: (0,) * a.ndim)
    per_seq = lambda a: pl.BlockSpec((1,) + a.shape[1:], lambda b, p, pt: (b,) + (0,) * (a.ndim - 1))
    grid_spec = pltpu.PrefetchScalarGridSpec(
        num_scalar_prefetch=1,
        grid=(bd, npg // n_pages),
        in_specs=[per_seq(qg)]
        + _page_specs(n_pages, (1, 1, 4, LANES, PAGE_SIZE), lambda p, g: p * n_pages + g, li)
        + [pl.BlockSpec((1, 1, 2, LANES, wb), lambda b, p, pt: (b, li, 0, 0, 0)),
           per_seq(new_rows), per_seq(gates), full(tab_t), full(expand), full(pe), full(w1bd), full(w2bd)],
        out_specs=pl.BlockSpec((1, NSA_H, LANES), lambda b, p, pt: (b, 0, 0)),
        scratch_shapes=[pltpu.VMEM((past, LANES), F32), pltpu.VMEM((past, LANES), F32),
                        pltpu.VMEM((LANES, past), BF16), pltpu.VMEM((LANES, past), BF16)])
    return pl.pallas_call(
        functools.partial(_nsa_decode_kernel, n_pages=n_pages, n_top=min(TOP_N, nb + 1)),
        grid_spec=grid_spec,
        out_shape=jax.ShapeDtypeStruct((bd, NSA_H, LANES), F32),
        compiler_params=_cparams("parallel", "arbitrary"),
        name="nsa_decode",
    )(page_table, qg, *([cache_t] * n_pages), win_t, new_rows, gates, tab_t, expand, pe, w1bd, w2bd)


def _prep_even_weights(w_in, b_f, cmp_pos, w1, w2):
    w = jnp.concatenate([w_in[:, 0:1280], w_in[:, 1304:1816], w_in[:, 1816:3352], w_in[:, 3360:3872],
                         w_in[:, 1280:1304], w_in[:, 3352:3360],
                         jnp.zeros((D_MODEL, LANES - SMALL_GATE - FOX_H), w_in.dtype)], axis=1).astype(BF16)
    bfp = jnp.zeros((1, LANES), F32).at[0, SMALL_GATE:SMALL_GATE + FOX_H].set(b_f)
    w1r = w1.reshape(2, CMP_BLOCK, HEAD_DIM, HEAD_DIM)
    z1 = jnp.zeros_like(w1r)
    w1bd = jnp.concatenate([jnp.concatenate([w1r, z1], -1), jnp.concatenate([z1, w1r], -1)], -2).astype(BF16)
    pe = jnp.concatenate([cmp_pos, cmp_pos], -1)
    z2 = jnp.zeros_like(w2)
    w2dup = jnp.concatenate([jnp.concatenate([w2, w2, z2, z2], -1), jnp.concatenate([z2, z2, w2, w2], -1)], -2).astype(BF16)
    w2bd = jnp.concatenate([jnp.concatenate([w2, z2], -1), jnp.concatenate([z2, w2], -1)], -2).astype(BF16)
    return w, bfp, pe, w1bd, w2dup, w2bd


def _head_rows(q, n_heads):
    bd = q.shape[0]
    q3 = q.reshape(bd, 1, n_heads, HEAD_DIM)
    eye = jnp.eye(n_heads, dtype=q.dtype).reshape(1, n_heads, n_heads, 1)
    return (q3 * eye).reshape(bd, n_heads, n_heads * HEAD_DIM)


def _group_rows(q):
    bd = q.shape[0]
    q3 = q.reshape(bd, NSA_H, HEAD_DIM)
    z = jnp.zeros_like(q3)
    first = (jnp.arange(NSA_H) < NSA_GROUP).reshape(1, NSA_H, 1)
    return jnp.where(first, jnp.concatenate([q3, z], -1), jnp.concatenate([z, q3], -1))


def kernel(x_prompt, x_sample, cache_nsa, cache_nsa_win, cache_fox, cache_fox_logf, cache_sb, page_table, norm_g, final_g, rel_bias, w_in_even, w_out_even, b_forget, cmp_pos, w_cmp1, w_cmp2, w_in_odd, w_out_odd):
    b, s, d = x_prompt.shape
    bd, ds_, _ = x_sample.shape
    assert ds_ == 1 and d == D_MODEL
    depth = norm_g.shape[0]
    n_even = w_in_even.shape[0]
    n_odd = w_in_odd.shape[0]
    n_phys = cache_nsa.shape[0]
    npg = page_table.shape[1]
    past = npg * PAGE_SIZE
    wb = cache_nsa_win.shape[2]
    t = min(ATT_T, s)
    tm_p = 256
    tm_o = 512

    xp = x_prompt.reshape(b * s, d)
    xs = x_sample.reshape(bd, d)
    gfin = final_g.reshape(1, d)
    rows_last = lambda a: jnp.transpose(a, (0, 1, 3, 4, 5, 2))
    cache_nsa_t = rows_last(cache_nsa).reshape(n_phys, n_even, 4, LANES, PAGE_SIZE)
    win_t = rows_last(cache_nsa_win).reshape(bd, n_even, 2, LANES, wb)
    cache_fox_t = rows_last(cache_fox).reshape(n_phys, n_even, 2, FOX_W, PAGE_SIZE)
    lft = jnp.transpose(cache_fox_logf.astype(F32), (0, 1, 3, 2))
    cache_sb_t = rows_last(cache_sb).reshape(n_phys, n_odd, 2, SB_W, PAGE_SIZE)
    n_pages = math.gcd(npg, DECODE_PAGES)
    tab_t = rel_bias.T
    expand = (jnp.arange(past)[None, :] // CMP_BLOCK == jnp.arange(past // CMP_BLOCK)[:, None]).astype(BF16)

    outs = {k: [] for k in ("nsa_s", "win_p", "win_s", "fox_s", "lf_p", "lf_s", "sb_s")}
    rows_t = fkv_t = kv_t = None
    for layer in range(depth):
        g = norm_g[layer].reshape(1, d)
        li = layer // 2
        final = layer == depth - 1
        if layer % 2 == 0:
            w, bfp, pe, w1bd, w2dup, w2bd = _prep_even_weights(w_in_even[li], b_forget[li], cmp_pos[li], w_cmp1[li],
                                                               w_cmp2[li])
            w_out = w_out_even[li].astype(BF16)
            stack = (li, n_even, b, w[:, E_ROWS:E_WIN].T, w[:, E_FK:E_FZ].T, (rows_t, fkv_t) if li > 0 else None)
            q, rows, selkv, win, winkv, small, nz, fq, fkv_t, fk, fv, fz, rows_t = _even_in(xp, g, w, bfp, tm_p, stack)
            r3 = lambda a: a.reshape(b, s, a.shape[-1])
            cmp = _compress_prompt(r3(rows), pe, w1bd, w2dup)
            o_n = _nsa_prompt(rel_bias, r3(q), r3(selkv), r3(winkv), cmp, r3(small), t)
            logf = r3(small)[:, :, SMALL_GATE:SMALL_GATE + FOX_H]
            c = _cumsum_lanes(jnp.transpose(logf, (0, 2, 1)))
            o_f = _fox_prompt(r3(fq), r3(fk), r3(fv), c.reshape(b, FOX_H // 2, 2, s), t)
            xp = _out_proj(xp, [(o_n.reshape(b * s, NSA_W), nz), (o_f.reshape(b * s, FOX_W), fz)], w_out, gfin, final,
                           tm_o)
            outs["win_p"].append(r3(win)[:, s - min(WINDOW, s):].reshape(b, min(WINDOW, s), 2, NSA_KVH, HEAD_DIM))
            outs["lf_p"].append(logf)
            q, rows, selkv, win, winkv, small, nz, fq, fkv, fk, fv, fz = _even_in(xs, g, w, bfp, bd)
            new_rows = jnp.stack([rows[:, 2 * LANES:3 * LANES], rows[:, 3 * LANES:], win[:, :LANES], win[:, LANES:]],
                                 axis=1)
            gates = small[:, :SMALL_GATE].reshape(bd, NSA_H, 3)
            o8 = _nsa_decode(page_table, _group_rows(q), cache_nsa_t, win_t, new_rows, gates, tab_t, expand, pe, w1bd,
                             w2bd, li, n_pages)
            o_n = o8[:, :, :HEAD_DIM].reshape(bd, NSA_W)
            logf = small[:, SMALL_GATE:SMALL_GATE + FOX_H]
            o_f = _fox_decode(page_table, _head_rows(fq, FOX_H), cache_fox_t, lft, fkv.reshape(bd, 1, 2 * FOX_W),
                              logf.reshape(bd, FOX_H, 1), li, n_pages).reshape(bd, FOX_W)
            xs = _out_proj(xs, [(o_n, nz), (o_f, fz)], w_out, gfin, final, bd)
            outs["nsa_s"].append(rows.reshape(bd, 1, 4, NSA_KVH, HEAD_DIM))
            outs["win_s"].append(jnp.concatenate([cache_nsa_win[:, li, 1:], win.reshape(bd, 1, 2, NSA_KVH, HEAD_DIM)],
                                                 axis=1))
            outs["fox_s"].append(fkv.reshape(bd, 1, 2, FOX_H, HEAD_DIM))
            outs["lf_s"].append(logf.reshape(bd, 1, FOX_H))
        else:
            w = w_in_odd[li].astype(BF16)
            w_out = w_out_odd[li].astype(BF16)
            q, kv_t, k, v, z = _odd_in(xp, g, w, tm_p, (li, n_odd, b, w[:, SB_W:3 * SB_W].T, kv_t if li > 0 else None))
            r3 = lambda a: a.reshape(b, s, a.shape[-1])
            o = _sb_prompt(r3(q), r3(k), r3(v), min(SB_T, s))
            xp = _out_proj(xp, [(o.reshape(b * s, SB_W), z)], w_out, gfin, final, tm_o)
            q, kv, k, v, z = _odd_in(xs, g, w, bd)
            o = _sb_decode(page_table, _head_rows(q, SB_H), cache_sb_t, li,
                           math.gcd(npg, SB_DECODE_PAGES)).reshape(bd, SB_W)
            xs = _out_proj(xs, [(o, z)], w_out, gfin, final, bd)
            outs["sb_s"].append(kv.reshape(bd, 1, 2, SB_H, HEAD_DIM))

    st = lambda k: jnp.stack(outs[k], 1)
    rows_of = lambda a, kinds, heads: jnp.transpose(a.reshape(b, a.shape[1], kinds, heads, HEAD_DIM, s), (0, 1, 5, 2, 3, 4))
    return (xp.reshape(b, s, d), xs.reshape(bd, 1, d), rows_of(rows_t, 4, NSA_KVH), st("nsa_s"), st("win_p"), st("win_s"),
            rows_of(fkv_t, 2, FOX_H), st("fox_s"), st("lf_p"), st("lf_s"), rows_of(kv_t, 2, SB_H), st("sb_s"))
```
